```python
import jax, jax.numpy as jnp
from jax import lax
import numpy as np

D_MODEL = 4096
BATCH = 4
SEQ = 4096
DEPTH = 2

N_MIXERS = 2
N_META = 16
GLA_HEADS = 4
GLA_DK = D_MODEL // 2 // GLA_HEADS
GLA_DV = D_MODEL // GLA_HEADS
GLA_QK = GLA_HEADS * GLA_DK
GLA_V = GLA_HEADS * GLA_DV
GLA_GATE_RANK = 16
GLA_GATE_TAU = 16.0
GLA_IN = 2 * GLA_QK + 2 * GLA_V + GLA_GATE_RANK
CHUNK = 64
CONV_WIDTH = 3
D_FF = -(-8 * D_MODEL // (3 * 256)) * 256
N_GLA = (DEPTH + 1) // 2
N_CONV = DEPTH // 2
EPS = 1e-6

kernel_name = "hybrid_gla_shortconv_meta_trunk"


def rms_norm(x, g):
    xf = x.astype(jnp.float32)
    y = xf * lax.rsqrt(jnp.mean(xf * xf, axis=-1, keepdims=True) + EPS)
    return (y * g.astype(jnp.float32)).astype(x.dtype)


def gla_mixer(h, w_in, w_a2, b_a, head_norm, w_out):
    B, L, _ = h.shape
    proj = h @ w_in
    q, k, v, r, a_low = jnp.split(proj, np.cumsum([GLA_QK, GLA_QK, GLA_V, GLA_V]).tolist(), axis=-1)
    log_a = jax.nn.log_sigmoid((a_low @ w_a2 + b_a).astype(jnp.float32)) / GLA_GATE_TAU
    q = q.astype(jnp.float32) * (GLA_DK ** -0.5)
    pad = CHUNK - N_META

    def to_chunks(t, d):
        t = jnp.pad(t.astype(jnp.float32), ((0, 0), (pad, 0), (0, 0)))
        n = t.shape[1] // CHUNK
        return t.reshape(B, n, CHUNK, GLA_HEADS, d).transpose(1, 0, 3, 2, 4)

    qc = to_chunks(q, GLA_DK)
    kc = to_chunks(k, GLA_DK)
    vc = to_chunks(v, GLA_DV)
    gc = to_chunks(log_a, GLA_DK)
    causal = jnp.tril(jnp.ones((CHUNK, CHUNK), dtype=bool))[None, None, :, :, None]

    def step(S, inp):
        qi, ki, vi, gi = inp
        b = jnp.cumsum(gi, axis=2)
        b_last = b[:, :, -1:, :]
        o_inter = jnp.einsum('bhik,bhkv->bhiv', qi * jnp.exp(b), S)
        diff = jnp.where(causal, b[:, :, :, None, :] - b[:, :, None, :, :], -jnp.inf)
        scores = jnp.sum(qi[:, :, :, None, :] * ki[:, :, None, :, :] * jnp.exp(diff), axis=-1)
        o_intra = jnp.einsum('bhij,bhjv->bhiv', scores, vi)
        S_new = jnp.exp(b_last[:, :, 0, :])[..., None] * S + jnp.einsum('bhjk,bhjv->bhkv', ki * jnp.exp(b_last - b), vi)
        return S_new, o_inter + o_intra

    S0 = jnp.zeros((B, GLA_HEADS, GLA_DK, GLA_DV), jnp.float32)
    _, o = lax.scan(step, S0, (qc, kc, vc, gc))
    o = o.transpose(1, 0, 3, 2, 4).reshape(B, -1, GLA_HEADS, GLA_DV)[:, pad:]
    o = o * lax.rsqrt(jnp.mean(o * o, axis=-1, keepdims=True) + EPS) * head_norm.astype(jnp.float32)
    o = o * jax.nn.silu(r.astype(jnp.float32)).reshape(B, L, GLA_HEADS, GLA_DV)
    return o.reshape(B, L, GLA_V).astype(h.dtype) @ w_out


def conv_mixer(h, w_in, conv_w, w_out):
    L = h.shape[1]
    bg, cg, u = jnp.split(h @ w_in, 3, axis=-1)
    z = cg * u
    zp = jnp.pad(z, ((0, 0), (CONV_WIDTH - 1, 0), (0, 0)))
    conv = conv_w[CONV_WIDTH - 1] * z
    for j in range(CONV_WIDTH - 1):
        conv = conv + conv_w[j] * zp[:, j:j + L]
    return (bg * conv) @ w_out


def swiglu(h, w_gate, w_up, w_down):
    return (jax.nn.silu(h @ w_gate) * (h @ w_up)) @ w_down


def setup_inputs(seed: int = 0) -> dict:
    key = jax.random.key(seed)
    ks = jax.random.split(key, 20)
    nrm = jax.random.normal
    f = jnp.float32
    return {
        "x": nrm(ks[0], (BATCH, SEQ, D_MODEL), f),
        "meta": nrm(ks[1], (N_META, D_MODEL), f),
        "norm_mix": 1.0 + 0.01 * nrm(ks[2], (DEPTH, D_MODEL), f),
        "norm_ffn": 1.0 + 0.01 * nrm(ks[3], (DEPTH, D_MODEL), f),
        "gla_w_in": nrm(ks[4], (N_GLA, D_MODEL, GLA_IN), f) * D_MODEL ** -0.5,
        "gla_w_a2": nrm(ks[5], (N_GLA, GLA_GATE_RANK, GLA_QK), f) * GLA_GATE_RANK ** -0.5,
        "gla_b_a": 0.01 * nrm(ks[6], (N_GLA, GLA_QK), f),
        "gla_head_norm": 1.0 + 0.01 * nrm(ks[7], (N_GLA, GLA_DV), f),
        "gla_w_out": nrm(ks[8], (N_GLA, GLA_V, D_MODEL), f) * GLA_V ** -0.5,
        "conv_w_in": nrm(ks[9], (N_CONV, D_MODEL, 3 * D_MODEL), f) * D_MODEL ** -0.5,
        "conv_w": nrm(ks[10], (N_CONV, CONV_WIDTH, D_MODEL), f) * CONV_WIDTH ** -0.5,
        "conv_w_out": nrm(ks[11], (N_CONV, D_MODEL, D_MODEL), f) * D_MODEL ** -0.5,
        "ffn_w_gate": nrm(ks[12], (DEPTH, D_MODEL, D_FF), f) * D_MODEL ** -0.5,
        "ffn_w_up": nrm(ks[13], (DEPTH, D_MODEL, D_FF), f) * D_MODEL ** -0.5,
        "ffn_w_down": nrm(ks[14], (DEPTH, D_FF, D_MODEL), f) * D_FF ** -0.5,
        "norm_final": 1.0 + 0.01 * nrm(ks[15], (D_MODEL,), f),
    }


def reference(x, meta, norm_mix, norm_ffn, gla_w_in, gla_w_a2, gla_b_a, gla_head_norm, gla_w_out,
              conv_w_in, conv_w, conv_w_out, ffn_w_gate, ffn_w_up, ffn_w_down, norm_final):
    B = x.shape[0]
    h = jnp.concatenate([jnp.broadcast_to(meta.astype(x.dtype)[None], (B, N_META, D_MODEL)), x], axis=1)
    for i in range(DEPTH):
        hn = rms_norm(h, norm_mix[i])
        j = i // N_MIXERS
        if i % N_MIXERS == 0:
            h = h + gla_mixer(hn, gla_w_in[j], gla_w_a2[j], gla_b_a[j], gla_head_norm[j], gla_w_out[j])
        else:
            h = h + conv_mixer(hn, conv_w_in[j], conv_w[j], conv_w_out[j])
        h = h + swiglu(rms_norm(h, norm_ffn[i]), ffn_w_gate[i], ffn_w_up[i], ffn_w_down[i])
    return rms_norm(h, norm_final)[:, N_META:]
```

```python
import functools

import jax
import jax.numpy as jnp
from jax import lax
from jax.experimental import pallas as pl
from jax.experimental.pallas import tpu as pltpu

_D = 4096
_N_META = 16
_HEADS = 4
_DK = 512
_DV = 1024
_QK = _HEADS * _DK
_V = _HEADS * _DV
_RANK = 16
_GATE_TAU = 16.0
_CONV_WIDTH = 3
_EPS = 1e-6

_LANES = 128
_BF16_SUBLANES = 16
_V7X_SCOPED_VMEM_BYTES = 60000 * 1024

_F32 = jnp.float32
_BF16 = jnp.bfloat16


def _params(block_bytes, scratch_bytes=0, n_grid=2):
    need = 2 * sum(block_bytes) + scratch_bytes
    assert need <= _V7X_SCOPED_VMEM_BYTES, (need, _V7X_SCOPED_VMEM_BYTES)
    return pltpu.CompilerParams(
        dimension_semantics=("arbitrary",) * n_grid,
        vmem_limit_bytes=_V7X_SCOPED_VMEM_BYTES,
    )


def _rmsnorm_kernel(x_ref, g_ref, o_ref):
    x = x_ref[...]
    ms = jnp.mean(x * x, axis=-1, keepdims=True)
    o_ref[...] = (x * lax.rsqrt(ms + _EPS) * g_ref[...]).astype(o_ref.dtype)


def _rmsnorm(x, g, out_dtype, bm=256):
    m, d = x.shape
    bm = min(bm, m)
    assert m % bm == 0
    blocks = [bm * d * 4, d * 4, bm * d * jnp.dtype(out_dtype).itemsize]
    return pl.pallas_call(
        _rmsnorm_kernel,
        grid=(m // bm,),
        in_specs=[pl.BlockSpec((bm, d), lambda i: (i, 0)), pl.BlockSpec((1, d), lambda i: (0, 0))],
        out_specs=pl.BlockSpec((bm, d), lambda i: (i, 0)),
        out_shape=jax.ShapeDtypeStruct((m, d), out_dtype),
        compiler_params=_params(blocks, scratch_bytes=2 * bm * d * 4, n_grid=1),
        name="rmsnorm",
    )(x, g.reshape(1, d))


def _matmul_kernel(a_ref, w_ref, o_ref):
    acc = jnp.dot(a_ref[...], w_ref[...], preferred_element_type=_F32)
    o_ref[...] = acc.astype(o_ref.dtype)


def _matmul_resid_kernel(a_ref, w_ref, r_ref, o_ref):
    acc = jnp.dot(a_ref[...], w_ref[...], preferred_element_type=_F32)
    o_ref[...] = (r_ref[...] + acc).astype(o_ref.dtype)


def _matmul(a, w, out_dtype, bm, bn, resid=None, name="matmul"):
    m, k = a.shape
    n = w.shape[1]
    bm, bn = min(bm, m), min(bn, n)
    assert m % bm == 0 and n % bn == 0
    out_bytes = jnp.dtype(out_dtype).itemsize
    blocks = [bm * k * 2, k * bn * 2, bm * bn * out_bytes]
    in_specs = [pl.BlockSpec((bm, k), lambda i, j: (i, 0)), pl.BlockSpec((k, bn), lambda i, j: (0, j))]
    args = [a, w]
    kernel = _matmul_kernel
    if resid is not None:
        in_specs.append(pl.BlockSpec((bm, bn), lambda i, j: (i, j)))
        args.append(resid)
        blocks.append(bm * bn * 4)
        kernel = _matmul_resid_kernel
    return pl.pallas_call(
        kernel,
        grid=(m // bm, n // bn),
        in_specs=in_specs,
        out_specs=pl.BlockSpec((bm, bn), lambda i, j: (i, j)),
        out_shape=jax.ShapeDtypeStruct((m, n), out_dtype),
        compiler_params=_params(blocks, scratch_bytes=bm * bn * 4),
        name=name,
    )(*args)


def _gate_up_kernel(a_ref, wg_ref, wu_ref, o_ref):
    a = a_ref[...]
    g = jnp.dot(a, wg_ref[...], preferred_element_type=_F32)
    u = jnp.dot(a, wu_ref[...], preferred_element_type=_F32)
    o_ref[...] = (g * jax.nn.sigmoid(g) * u).astype(o_ref.dtype)


def _gate_up(a, wg, wu, bm, bn):
    m, k = a.shape
    n = wg.shape[1]
    bm = min(bm, m)
    assert m % bm == 0 and n % bn == 0
    blocks = [bm * k * 2, k * bn * 2, k * bn * 2, bm * bn * 2]
    return pl.pallas_call(
        _gate_up_kernel,
        grid=(m // bm, n // bn),
        in_specs=[
            pl.BlockSpec((bm, k), lambda i, j: (i, 0)),
            pl.BlockSpec((k, bn), lambda i, j: (0, j)),
            pl.BlockSpec((k, bn), lambda i, j: (0, j)),
        ],
        out_specs=pl.BlockSpec((bm, bn), lambda i, j: (i, j)),
        out_shape=jax.ShapeDtypeStruct((m, n), _BF16),
        compiler_params=_params(blocks, scratch_bytes=3 * bm * bn * 4),
        name="ffn_gate_up",
    )(a, wg, wu)


def _cumsum_rows(tri, g):
    g1 = g.astype(_BF16)
    r1 = g - g1.astype(_F32)
    g2 = r1.astype(_BF16)
    g3 = (r1 - g2.astype(_F32)).astype(_BF16)
    out = jnp.dot(tri, g1, preferred_element_type=_F32)
    out = out + jnp.dot(tri, g2, preferred_element_type=_F32)
    return out + jnp.dot(tri, g3, preferred_element_type=_F32)


def _gla_kernel(q_ref, k_ref, v_ref, r_ref, al_ref, wa2_ref, ba_ref, hnorm_ref, s0_ref,
                o_ref, sfin_ref, s_scr, *, chunk, n_chunks, sub):
    step = pl.program_id(2)

    @pl.when(step == 0)
    def _load_state():
        s_scr[...] = s0_ref[0]

    c = chunk
    n_sub = c // sub
    tri = (lax.broadcasted_iota(jnp.int32, (c, c), 0) >= lax.broadcasted_iota(jnp.int32, (c, c), 1)).astype(_BF16)
    sub_row = lax.broadcasted_iota(jnp.int32, (sub, c), 0)
    sub_col = lax.broadcasted_iota(jnp.int32, (sub, c), 1)

    def one_chunk(ci, carry):
        rows = pl.ds(pl.multiple_of(ci * c, c), c)
        a_low = al_ref[rows, :].astype(_BF16)
        x = jnp.dot(a_low, wa2_ref[...], preferred_element_type=_F32) + ba_ref[...]
        g = (jnp.minimum(x, 0.0) - jnp.log1p(jnp.exp(-jnp.abs(x)))) * (1.0 / _GATE_TAU)
        b = _cumsum_rows(tri, g)
        b_last = b[c - 1:c, :]
        qs = q_ref[rows, :].astype(_F32) * (_DK ** -0.5)
        kf = k_ref[rows, :].astype(_F32)
        v = v_ref[rows, :]
        s = s_scr[...]

        o = jnp.dot((qs * jnp.exp(b)).astype(_BF16), s.astype(_BF16), preferred_element_type=_F32)

        tiles = []
        for blk in range(n_sub):
            r0 = blk * sub
            b_b, q_b, k_b = b[r0:r0 + sub], qs[r0:r0 + sub], kf[r0:r0 + sub]
            sc = jnp.zeros((sub, c), _F32)
            for j in range(sub):
                decay = jnp.exp(jnp.minimum(b_b - b_b[j:j + 1], 0.0))
                col = jnp.sum(q_b * decay * k_b[j:j + 1], axis=-1, keepdims=True)
                sc = jnp.where(sub_col == r0 + j, col, sc)
            sc = jnp.where(sub_row + r0 >= sub_col, sc, 0.0)
            if blk > 0:
                p = b[r0 - 1:r0]
                q_o = (q_b * jnp.exp(b_b - p)).astype(_BF16)
                k_o = (kf * jnp.exp(jnp.minimum(p - b, 0.0))).astype(_BF16)
                off = lax.dot_general(q_o, k_o, (((1,), (1,)), ((), ())), preferred_element_type=_F32)
                sc = sc + jnp.where(sub_col < r0, off, 0.0)
            tiles.append(sc)
        scores = jnp.concatenate(tiles, axis=0) if n_sub > 1 else tiles[0]
        o = o + jnp.dot(scores.astype(_BF16), v, preferred_element_type=_F32)

        k_dec = (kf * jnp.exp(b_last - b)).astype(_BF16)
        upd = lax.dot_general(k_dec, v, (((0,), (0,)), ((), ())), preferred_element_type=_F32)
        decay_col = jnp.transpose(jnp.broadcast_to(jnp.exp(b_last), (_LANES, _DK)))
        s_scr[...] = s * jnp.tile(decay_col, (1, _DV // _LANES)) + upd

        ms = jnp.mean(o * o, axis=-1, keepdims=True)
        o_n = o * lax.rsqrt(ms + _EPS) * hnorm_ref[...]
        r = r_ref[rows, :].astype(_F32)
        o_ref[rows, :] = (o_n * (r * jax.nn.sigmoid(r))).astype(o_ref.dtype)
        return carry

    lax.fori_loop(0, n_chunks, one_chunk, 0)

    @pl.when(step == pl.num_programs(2) - 1)
    def _store_state():
        sfin_ref[0, 0] = s_scr[...]


def _gla(proj, a_low, wa2, ba, hnorm, s0, *, n_batch, seq, chunk, rows_per_step):
    m = proj.shape[0]
    assert m == n_batch * seq and seq % rows_per_step == 0 and rows_per_step % chunk == 0
    steps = seq // rows_per_step
    r = rows_per_step
    kv = _QK // _DV

    def row(b, h, i):
        return b * steps + i

    kernel = functools.partial(_gla_kernel, chunk=chunk, n_chunks=r // chunk, sub=min(chunk, 16))
    blocks = [r * _DK * 2, r * _DK * 2, r * _DV * 2, r * _DV * 2, r * _LANES * 4, _LANES * _DK * 2,
              _DK * 4, _DV * 4, _DK * _DV * 4, r * _DV * 2, _DK * _DV * 4]
    return pl.pallas_call(
        kernel,
        grid=(n_batch, _HEADS, steps),
        in_specs=[
            pl.BlockSpec((r, _DK), lambda b, h, i: (row(b, h, i), h)),
            pl.BlockSpec((r, _DK), lambda b, h, i: (row(b, h, i), _HEADS + h)),
            pl.BlockSpec((r, _DV), lambda b, h, i: (row(b, h, i), 2 * kv + h)),
            pl.BlockSpec((r, _DV), lambda b, h, i: (row(b, h, i), 2 * kv + _HEADS + h)),
            pl.BlockSpec((r, _LANES), lambda b, h, i: (row(b, h, i), 0)),
            pl.BlockSpec((_LANES, _DK), lambda b, h, i: (0, h)),
            pl.BlockSpec((1, _DK), lambda b, h, i: (0, h)),
            pl.BlockSpec((1, _DV), lambda b, h, i: (0, 0)),
            pl.BlockSpec((1, _DK, _DV), lambda b, h, i: (h, 0, 0)),
        ],
        out_specs=[
            pl.BlockSpec((r, _DV), lambda b, h, i: (row(b, h, i), h)),
            pl.BlockSpec((1, 1, _DK, _DV), lambda b, h, i: (b, h, 0, 0)),
        ],
        out_shape=[
            jax.ShapeDtypeStruct((m, _V), _BF16),
            jax.ShapeDtypeStruct((n_batch, _HEADS, _DK, _DV), _F32),
        ],
        scratch_shapes=[pltpu.VMEM((_DK, _DV), _F32)],
        compiler_params=_params(blocks, scratch_bytes=4 * _DK * _DV * 4, n_grid=3),
        name="gla",
    )(proj, proj, proj, proj, a_low, wa2, ba.reshape(1, _QK), hnorm.reshape(1, _DV), s0)


def _conv_kernel(bg_ref, cg_ref, u_ref, cgp_ref, up_ref, cgm_ref, um_ref, w_ref, o_ref):
    first = pl.program_id(1) == 0
    z = cg_ref[...].astype(_F32) * u_ref[...].astype(_F32)
    prev = cgp_ref[...].astype(_F32) * up_ref[...].astype(_F32)
    meta = cgm_ref[...].astype(_F32) * um_ref[...].astype(_F32)
    halo = jnp.where(first, meta, prev)
    n_halo = halo.shape[0]
    z_m1, z_m2 = halo[n_halo - 1:n_halo], halo[n_halo - 2:n_halo - 1]
    rid = lax.broadcasted_iota(jnp.int32, z.shape, 0)
    z1 = jnp.where(rid == 0, z_m1, pltpu.roll(z, 1, 0))
    z2 = jnp.where(rid == 0, z_m2, jnp.where(rid == 1, z_m1, pltpu.roll(z, 2, 0)))
    w = w_ref[...]
    conv = w[2:3] * z + w[0:1] * z2 + w[1:2] * z1
    o_ref[...] = (bg_ref[...].astype(_F32) * conv).astype(o_ref.dtype)


def _conv_mix(proj, proj_meta, conv_w, *, n_batch, seq, rows, cols):
    m = proj.shape[0]
    assert seq % rows == 0 and _D % cols == 0 and rows % _BF16_SUBLANES == 0
    steps = seq // rows
    nc = _D // cols
    hb = rows // _BF16_SUBLANES

    def prev_rows(b, i, j):
        return jnp.maximum((b * steps + i) * hb - 1, 0)

    blocks = [rows * cols * 2] * 3 + [_BF16_SUBLANES * cols * 2] * 4 + [_CONV_WIDTH * cols * 4, rows * cols * 2]
    return pl.pallas_call(
        _conv_kernel,
        grid=(n_batch, steps, nc),
        in_specs=[
            pl.BlockSpec((rows, cols), lambda b, i, j: (b * steps + i, j)),
            pl.BlockSpec((rows, cols), lambda b, i, j: (b * steps + i, nc + j)),
            pl.BlockSpec((rows, cols), lambda b, i, j: (b * steps + i, 2 * nc + j)),
            pl.BlockSpec((_BF16_SUBLANES, cols), lambda b, i, j: (prev_rows(b, i, j), nc + j)),
            pl.BlockSpec((_BF16_SUBLANES, cols), lambda b, i, j: (prev_rows(b, i, j), 2 * nc + j)),
            pl.BlockSpec((_N_META, cols), lambda b, i, j: (0, nc + j)),
            pl.BlockSpec((_N_META, cols), lambda b, i, j: (0, 2 * nc + j)),
            pl.BlockSpec((_CONV_WIDTH, cols), lambda b, i, j: (0, j)),
        ],
        out_specs=pl.BlockSpec((rows, cols), lambda b, i, j: (b * steps + i, j)),
        out_shape=jax.ShapeDtypeStruct((m, _D), _BF16),
        compiler_params=_params(blocks, scratch_bytes=8 * rows * cols * 4, n_grid=3),
        name="conv_mix",
    )(proj, proj, proj, proj, proj, proj_meta, proj_meta, conv_w)


def _ffn(h, norm_g, wg, wu, wd, *, bm_gu, bm_down):
    hn = _rmsnorm(h, norm_g, _BF16)
    act = _gate_up(hn, wg, wu, bm=bm_gu, bn=256)
    return _matmul(act, wd, _F32, bm=bm_down, bn=512, resid=h, name="ffn_down")


def kernel(x, meta, norm_mix, norm_ffn, gla_w_in, gla_w_a2, gla_b_a, gla_head_norm, gla_w_out,
           conv_w_in, conv_w, conv_w_out, ffn_w_gate, ffn_w_up, ffn_w_down, norm_final):
    n_batch, seq, d = x.shape
    xm = x.reshape(n_batch * seq, d)
    streams = {"x": xm, "meta": meta.astype(x.dtype)}

    w_in = gla_w_in[0]
    w_qkvr = w_in[:, :2 * _QK + 2 * _V].astype(_BF16)
    w_a1 = jnp.pad(w_in[:, 2 * _QK + 2 * _V:], ((0, 0), (0, _LANES - _RANK))).astype(_BF16)
    w_a2 = jnp.pad(gla_w_a2[0], ((0, _LANES - _RANK), (0, 0))).astype(_BF16)
    w_out0 = gla_w_out[0].astype(_BF16)
    w_cin = conv_w_in[0].astype(_BF16)
    w_cout = conv_w_out[0].astype(_BF16)
    w_gate = [ffn_w_gate[i].astype(_BF16) for i in range(2)]
    w_up = [ffn_w_up[i].astype(_BF16) for i in range(2)]
    w_down = [ffn_w_down[i].astype(_BF16) for i in range(2)]

    proj, a_low = {}, {}
    for name, h in streams.items():
        hn = _rmsnorm(h, norm_mix[0], _BF16)
        proj[name] = _matmul(hn, w_qkvr, _BF16, bm=1024, bn=1024, name="gla_in")
        a_low[name] = _matmul(hn, w_a1, _F32, bm=1024, bn=_LANES, name="gla_gate_in")
    zero_state = jnp.zeros((_HEADS, _DK, _DV), _F32)
    o_meta, s_meta = _gla(proj["meta"], a_low["meta"], w_a2, gla_b_a[0], gla_head_norm[0], zero_state,
                          n_batch=1, seq=_N_META, chunk=_N_META, rows_per_step=_N_META)
    o_x, _ = _gla(proj["x"], a_low["x"], w_a2, gla_b_a[0], gla_head_norm[0], s_meta[0],
                  n_batch=n_batch, seq=seq, chunk=64, rows_per_step=512)
    mixed = {"x": o_x, "meta": o_meta}
    for name in streams:
        h = _matmul(mixed[name], w_out0, _F32, bm=1024, bn=1024, resid=streams[name], name="gla_out")
        streams[name] = _ffn(h, norm_ffn[0], w_gate[0], w_up[0], w_down[0], bm_gu=1024, bm_down=512)

    proj = {}
    for name, h in streams.items():
        hn = _rmsnorm(h, norm_mix[1], _BF16)
        proj[name] = _matmul(hn, w_cin, _BF16, bm=1024, bn=1024, name="conv_in")
    y = _conv_mix(proj["x"], proj["meta"], conv_w[0], n_batch=n_batch, seq=seq, rows=512, cols=1024)
    h = _matmul(y, w_cout, _F32, bm=1024, bn=1024, resid=streams["x"], name="conv_out")
    h = _ffn(h, norm_ffn[1], w_gate[1], w_up[1], w_down[1], bm_gu=1024, bm_down=512)
    out = _rmsnorm(h, norm_final, _F32)
    return out.reshape(n_batch, seq, d)
```

```python
import functools
from typing import NamedTuple

import jax
import jax.numpy as jnp
from jax import lax
from jax.experimental import pallas as pl
from jax.experimental.pallas import tpu as pltpu

_D = 4096
_N_META = 16
_HEADS = 4
_DK = 512
_DV = 1024
_QK = _HEADS * _DK
_V = _HEADS * _DV
_RANK = 16
_GATE_TAU = 16.0
_CONV_WIDTH = 3
_EPS = 1e-6
_LOG2E = 1.4426950408889634

_LANES = 128
_SUBLANES = 8
_BF16_SUBLANES = 16
_V7X_SCOPED_VMEM_BYTES = 60000 * 1024

_F32 = jnp.float32
_BF16 = jnp.bfloat16


def _params(block_bytes, scratch_bytes=0, n_grid=2):
    need = 2 * sum(block_bytes) + scratch_bytes
    assert need <= _V7X_SCOPED_VMEM_BYTES, (need, _V7X_SCOPED_VMEM_BYTES)
    return pltpu.CompilerParams(
        dimension_semantics=("arbitrary",) * n_grid,
        vmem_limit_bytes=_V7X_SCOPED_VMEM_BYTES,
    )


class _CastJob(NamedTuple):
    src: jax.Array
    layer: int
    n_blocks: int

    @property
    def block(self):
        rows, cols = self.src.shape[1:]
        assert rows % self.n_blocks == 0 and (rows // self.n_blocks) % _BF16_SUBLANES == 0
        return rows // self.n_blocks, cols

    def specs(self, flat_step):
        rb, cols = self.block
        last = self.n_blocks - 1
        layer = self.layer
        src = pl.BlockSpec((1, rb, cols), lambda *g: (layer, jnp.minimum(flat_step(*g), last), 0))
        dst = pl.BlockSpec((rb, cols), lambda *g: (jnp.minimum(flat_step(*g), last), 0))
        return src, dst

    @property
    def out_shape(self):
        return jax.ShapeDtypeStruct(self.src.shape[1:], _BF16)

    @property
    def block_bytes(self):
        rb, cols = self.block
        return [rb * cols * 4, rb * cols * 2]


def _run_casts(step, job_blocks, src_refs, dst_refs):
    for n_blocks, src, dst in zip(job_blocks, src_refs, dst_refs):
        @pl.when(step < n_blocks)
        def _cast(src=src, dst=dst):
            dst[...] = src[0].astype(_BF16)


def _job_plumbing(jobs, n_steps, flat_step):
    in_specs, out_specs, out_shapes, blocks = [], [], [], []
    for job in jobs:
        assert job.n_blocks <= n_steps, (job.n_blocks, n_steps)
        src, dst = job.specs(flat_step)
        in_specs.append(src)
        out_specs.append(dst)
        out_shapes.append(job.out_shape)
        blocks += job.block_bytes
    return in_specs, out_specs, out_shapes, blocks


def _cast_cols_kernel(src_ref, dst_ref):
    dst_ref[...] = src_ref[0].astype(_BF16)


def _cast_cols(src, layer, n_cols, rb, cb):
    rows = src.shape[1]
    assert rows % rb == 0 and n_cols % cb == 0
    return pl.pallas_call(
        _cast_cols_kernel,
        grid=(rows // rb, n_cols // cb),
        in_specs=[pl.BlockSpec((1, rb, cb), lambda i, j: (layer, i, j))],
        out_specs=pl.BlockSpec((rb, cb), lambda i, j: (i, j)),
        out_shape=jax.ShapeDtypeStruct((rows, n_cols), _BF16),
        compiler_params=_params([rb * cb * 4, rb * cb * 2]),
        name="weight_cast",
    )(src)


def _rmsnorm_kernel(x_ref, g_ref, o_ref):
    x = x_ref[...]
    ms = jnp.mean(x * x, axis=-1, keepdims=True)
    o_ref[...] = (x * lax.rsqrt(ms + _EPS) * g_ref[...]).astype(o_ref.dtype)


def _rmsnorm(x, g, out_dtype, bm=256):
    m, d = x.shape
    bm = min(bm, m)
    assert m % bm == 0
    blocks = [bm * d * 4, d * 4, bm * d * jnp.dtype(out_dtype).itemsize]
    return pl.pallas_call(
        _rmsnorm_kernel,
        grid=(m // bm,),
        in_specs=[pl.BlockSpec((bm, d), lambda i: (i, 0)), pl.BlockSpec((1, d), lambda i: (0, 0))],
        out_specs=pl.BlockSpec((bm, d), lambda i: (i, 0)),
        out_shape=jax.ShapeDtypeStruct((m, d), out_dtype),
        compiler_params=_params(blocks, scratch_bytes=2 * bm * d * 4, n_grid=1),
        name="rmsnorm",
    )(x, g.reshape(1, d))


def _epilogue_cast(accs, resid):
    return accs[0]


def _epilogue_resid(accs, resid):
    return resid + accs[0]


def _epilogue_swiglu(accs, resid):
    gate, up = accs
    return gate * jax.nn.sigmoid(gate) * up


def _matmul_kernel(*refs, n_w, epilogue, has_resid, has_meta, jobs, n_inner):
    it = iter(refs)
    a_ref = next(it)
    w_refs = [next(it) for _ in range(n_w)]
    r_ref = next(it) if has_resid else None
    am_ref = next(it) if has_meta else None
    rm_ref = next(it) if has_meta and has_resid else None
    src_refs = [next(it) for _ in jobs]
    o_ref = next(it)
    om_ref = next(it) if has_meta else None
    dst_refs = [next(it) for _ in jobs]
    i, j = pl.program_id(0), pl.program_id(1)

    a = a_ref[...]
    accs = [jnp.dot(a, w[...], preferred_element_type=_F32) for w in w_refs]
    o_ref[...] = epilogue(accs, r_ref[...] if has_resid else None).astype(o_ref.dtype)

    if has_meta:
        @pl.when(i == 0)
        def _meta_rows():
            am = am_ref[...]
            accs_m = [jnp.dot(am, w[...], preferred_element_type=_F32) for w in w_refs]
            om_ref[0] = epilogue(accs_m, rm_ref[...] if has_resid else None).astype(om_ref.dtype)

        @pl.when(i != 0)
        def _meta_unused():
            om_ref[0] = jnp.zeros(om_ref.shape[1:], om_ref.dtype)

    _run_casts(i * n_inner + j, jobs, src_refs, dst_refs)


def _matmul(a, ws, n_out, epilogue, out_dtype, bm, bn, resid=None, meta=None, jobs=(), name="matmul"):
    m, k = a.shape
    bm = min(bm, m)
    assert m % bm == 0 and n_out % bn == 0
    n_i, n_j = m // bm, n_out // bn
    out_bytes = jnp.dtype(out_dtype).itemsize
    has_resid, has_meta = resid is not None, meta is not None

    in_specs = [pl.BlockSpec((bm, k), lambda i, j: (i, 0))]
    args = [a]
    blocks = [bm * k * 2, bm * bn * out_bytes]
    for w, off in ws:
        in_specs.append(pl.BlockSpec((k, bn), lambda i, j, off=off: (0, off + j)))
        args.append(w)
        blocks.append(k * bn * 2)
    if has_resid:
        in_specs.append(pl.BlockSpec((bm, bn), lambda i, j: (i, j)))
        args.append(resid)
        blocks.append(bm * bn * 4)
    out_specs = [pl.BlockSpec((bm, bn), lambda i, j: (i, j))]
    out_shapes = [jax.ShapeDtypeStruct((m, n_out), out_dtype)]
    if has_meta:
        a_meta, r_meta = meta
        in_specs.append(pl.BlockSpec((_N_META, k), lambda i, j: (0, 0)))
        args.append(a_meta)
        blocks.append(_N_META * k * 2)
        if has_resid:
            in_specs.append(pl.BlockSpec((_N_META, bn), lambda i, j: (0, j)))
            args.append(r_meta)
        out_specs.append(pl.BlockSpec((1, _N_META, bn), lambda i, j: (i, 0, j)))
        out_shapes.append(jax.ShapeDtypeStruct((n_i, _N_META, n_out), out_dtype))
    j_in, j_out, j_shapes, j_blocks = _job_plumbing(jobs, n_i * n_j, lambda i, j: i * n_j + j)
    for job in jobs:
        args.append(job.src)

    kernel = functools.partial(_matmul_kernel, n_w=len(ws), epilogue=epilogue, has_resid=has_resid,
                               has_meta=has_meta, jobs=tuple(job.n_blocks for job in jobs), n_inner=n_j)
    outs = pl.pallas_call(
        kernel,
        grid=(n_i, n_j),
        in_specs=in_specs + j_in,
        out_specs=out_specs + j_out,
        out_shape=out_shapes + j_shapes,
        compiler_params=_params(blocks + j_blocks, scratch_bytes=(len(ws) + 1) * bm * bn * 4),
        name=name,
    )(*args)
    out = outs[0]
    out_meta = outs[1][0] if has_meta else None
    return out, out_meta, list(outs[1 + has_meta:])


def _cumsum_rows(tri, g):
    g1 = g.astype(_BF16)
    r1 = g - g1.astype(_F32)
    g2 = r1.astype(_BF16)
    g3 = (r1 - g2.astype(_F32)).astype(_BF16)
    out = jnp.dot(tri, g1, preferred_element_type=_F32)
    out = out + jnp.dot(tri, g2, preferred_element_type=_F32)
    return out + jnp.dot(tri, g3, preferred_element_type=_F32)


def _gla_kernel(*refs, chunk, n_chunks, sub, jobs, n_steps):
    q_ref, k_ref, v_ref, r_ref, al_ref, wa2_ref, ba_ref, hnorm_ref, s0_ref = refs[:9]
    src_refs = refs[9:9 + len(jobs)]
    o_ref, sfin_ref = refs[9 + len(jobs):11 + len(jobs)]
    dst_refs = refs[11 + len(jobs):11 + 2 * len(jobs)]
    s_scr = refs[-1]
    step = pl.program_id(2)

    @pl.when(step == 0)
    def _load_state():
        s_scr[...] = s0_ref[0]

    c = chunk
    n_sub = c // sub
    tri = (lax.broadcasted_iota(jnp.int32, (c, c), 0) >= lax.broadcasted_iota(jnp.int32, (c, c), 1)).astype(_BF16)
    sub_row = lax.broadcasted_iota(jnp.int32, (sub, c), 0)
    sub_col = lax.broadcasted_iota(jnp.int32, (sub, c), 1)

    def one_chunk(ci, carry):
        rows = pl.ds(pl.multiple_of(ci * c, c), c)
        a_low = al_ref[rows, :].astype(_BF16)
        x = jnp.dot(a_low, wa2_ref[...], preferred_element_type=_F32) + ba_ref[...]
        g = (jnp.minimum(x, 0.0) - jnp.log1p(jnp.exp(-jnp.abs(x)))) * (_LOG2E / _GATE_TAU)
        b = _cumsum_rows(tri, g)
        b_last = b[c - 1:c, :]
        k_b16 = k_ref[rows, :]
        qs = q_ref[rows, :].astype(_F32) * (_DK ** -0.5)
        kf = k_b16.astype(_F32)
        v = v_ref[rows, :]
        s = s_scr[...]

        o = jnp.dot((qs * jnp.exp2(b)).astype(_BF16), s.astype(_BF16), preferred_element_type=_F32)

        tiles = []
        for blk in range(n_sub):
            r0 = blk * sub
            b_b, q_b = b[r0:r0 + sub], qs[r0:r0 + sub]
            stacked = jnp.concatenate(
                [(q_b * jnp.exp2(jnp.minimum(b_b - b_b[j:j + 1], 0.0))).astype(_BF16) for j in range(sub)], axis=0)
            full = lax.dot_general(stacked, k_b16, (((1,), (1,)), ((), ())), preferred_element_type=_F32)
            sc = jnp.zeros((sub, c), _F32)
            for j in range(sub):
                sc = jnp.where(sub_col == r0 + j, full[j * sub:(j + 1) * sub], sc)
            sc = jnp.where(sub_row + r0 >= sub_col, sc, 0.0)
            if blk > 0:
                p = b[r0 - 1:r0]
                q_o = (q_b * jnp.exp2(b_b - p)).astype(_BF16)
                k_o = (kf * jnp.exp2(jnp.minimum(p - b, 0.0))).astype(_BF16)
                off = lax.dot_general(q_o, k_o, (((1,), (1,)), ((), ())), preferred_element_type=_F32)
                sc = sc + jnp.where(sub_col < r0, off, 0.0)
            tiles.append(sc)
        scores = jnp.concatenate(tiles, axis=0) if n_sub > 1 else tiles[0]
        o = o + jnp.dot(scores.astype(_BF16), v, preferred_element_type=_F32)

        k_dec = (kf * jnp.exp2(b_last - b)).astype(_BF16)
        upd = lax.dot_general(k_dec, v, (((0,), (0,)), ((), ())), preferred_element_type=_F32)
        decay_col = jnp.transpose(jnp.broadcast_to(jnp.exp2(b_last), (_LANES, _DK)))
        s_scr[...] = s * jnp.tile(decay_col, (1, _DV // _LANES)) + upd

        ms = jnp.mean(o * o, axis=-1, keepdims=True)
        o_n = o * lax.rsqrt(ms + _EPS) * hnorm_ref[...]
        r = r_ref[rows, :].astype(_F32)
        o_ref[rows, :] = (o_n * (r * jax.nn.sigmoid(r))).astype(o_ref.dtype)
        return carry

    lax.fori_loop(0, n_chunks, one_chunk, 0)

    @pl.when(step == pl.num_programs(2) - 1)
    def _store_state():
        sfin_ref[0, 0] = s_scr[...]

    flat = (pl.program_id(0) * _HEADS + pl.program_id(1)) * n_steps + step
    _run_casts(flat, jobs, src_refs, dst_refs)


def _gla(proj, a_low, wa2, ba, hnorm, s0, *, n_batch, seq, chunk, rows_per_step, jobs=()):
    m = proj.shape[0]
    assert m == n_batch * seq and seq % rows_per_step == 0 and rows_per_step % chunk == 0
    steps = seq // rows_per_step
    r = rows_per_step
    kv = _QK // _DV

    def row(b, h, i):
        return b * steps + i

    j_in, j_out, j_shapes, j_blocks = _job_plumbing(
        jobs, n_batch * _HEADS * steps, lambda b, h, i: (b * _HEADS + h) * steps + i)
    kernel = functools.partial(_gla_kernel, chunk=chunk, n_chunks=r // chunk, sub=min(chunk, 16),
                               jobs=tuple(job.n_blocks for job in jobs), n_steps=steps)
    blocks = [r * _DK * 2, r * _DK * 2, r * _DV * 2, r * _DV * 2, r * _LANES * 4, _LANES * _DK * 2,
              _DK * 4, _DV * 4, _DK * _DV * 4, r * _DV * 2, _DK * _DV * 4]
    outs = pl.pallas_call(
        kernel,
        grid=(n_batch, _HEADS, steps),
        in_specs=[
            pl.BlockSpec((r, _DK), lambda b, h, i: (row(b, h, i), h)),
            pl.BlockSpec((r, _DK), lambda b, h, i: (row(b, h, i), _HEADS + h)),
            pl.BlockSpec((r, _DV), lambda b, h, i: (row(b, h, i), 2 * kv + h)),
            pl.BlockSpec((r, _DV), lambda b, h, i: (row(b, h, i), 2 * kv + _HEADS + h)),
            pl.BlockSpec((r, _LANES), lambda b, h, i: (row(b, h, i), 0)),
            pl.BlockSpec((_LANES, _DK), lambda b, h, i: (0, h)),
            pl.BlockSpec((1, _DK), lambda b, h, i: (0, h)),
            pl.BlockSpec((1, _DV), lambda b, h, i: (0, 0)),
            pl.BlockSpec((1, _DK, _DV), lambda b, h, i: (h, 0, 0)),
        ] + j_in,
        out_specs=[
            pl.BlockSpec((r, _DV), lambda b, h, i: (row(b, h, i), h)),
            pl.BlockSpec((1, 1, _DK, _DV), lambda b, h, i: (b, h, 0, 0)),
        ] + j_out,
        out_shape=[
            jax.ShapeDtypeStruct((m, _V), _BF16),
            jax.ShapeDtypeStruct((n_batch, _HEADS, _DK, _DV), _F32),
        ] + j_shapes,
        scratch_shapes=[pltpu.VMEM((_DK, _DV), _F32)],
        compiler_params=_params(blocks + j_blocks, scratch_bytes=4 * _DK * _DV * 4, n_grid=3),
        name="gla",
    )(proj, proj, proj, proj, a_low, wa2, ba.reshape(1, _QK), hnorm.reshape(1, _DV), s0, *[j.src for j in jobs])
    return outs[0], outs[1], list(outs[2:])


def _conv_in_kernel(*refs, jobs, n_inner, blocks_per_seq):
    a_ref, wb_ref, wc_ref, wu_ref, am_ref, cw_ref = refs[:6]
    src_refs = refs[6:6 + len(jobs)]
    o_ref = refs[6 + len(jobs)]
    dst_refs = refs[7 + len(jobs):7 + 2 * len(jobs)]
    tail_scr, meta_scr = refs[-2:]
    i, j = pl.program_id(0), pl.program_id(1)

    @pl.when(i == 0)
    def _meta_rows():
        am = am_ref[...]
        z_meta = (jnp.dot(am, wc_ref[...], preferred_element_type=_F32)
                  * jnp.dot(am, wu_ref[...], preferred_element_type=_F32))
        meta_scr[j] = z_meta[_N_META - _SUBLANES:]
        tail_scr[j] = z_meta[_N_META - _SUBLANES:]

    a = a_ref[...]
    bg = jnp.dot(a, wb_ref[...], preferred_element_type=_F32)
    z = jnp.dot(a, wc_ref[...], preferred_element_type=_F32) * jnp.dot(a, wu_ref[...], preferred_element_type=_F32)
    prev = jnp.where(i % blocks_per_seq == 0, meta_scr[j], tail_scr[j])
    tail_scr[j] = z[z.shape[0] - _SUBLANES:]
    z_m1, z_m2 = prev[_SUBLANES - 1:_SUBLANES], prev[_SUBLANES - 2:_SUBLANES - 1]
    rid = lax.broadcasted_iota(jnp.int32, z.shape, 0)
    z1 = jnp.where(rid == 0, z_m1, pltpu.roll(z, 1, 0))
    z2 = jnp.where(rid == 0, z_m2, jnp.where(rid == 1, z_m1, pltpu.roll(z, 2, 0)))
    cw = cw_ref[...]
    conv = cw[2:3] * z + cw[0:1] * z2 + cw[1:2] * z1
    o_ref[...] = (bg * conv).astype(o_ref.dtype)

    _run_casts(i * n_inner + j, jobs, src_refs, dst_refs)


def _conv_in(a, a_meta, w, conv_w, *, seq, bm, bn, jobs=()):
    m, k = a.shape
    assert m % bm == 0 and seq % bm == 0 and _D % bn == 0
    n_i, n_j = m // bm, _D // bn
    j_in, j_out, j_shapes, j_blocks = _job_plumbing(jobs, n_i * n_j, lambda i, j: i * n_j + j)
    kernel = functools.partial(_conv_in_kernel, jobs=tuple(job.n_blocks for job in jobs), n_inner=n_j, blocks_per_seq=seq // bm)
    blocks = [bm * k * 2, 3 * k * bn * 2, _N_META * k * 2, _CONV_WIDTH * bn * 4, bm * bn * 2]
    outs = pl.pallas_call(
        kernel,
        grid=(n_i, n_j),
        in_specs=[
            pl.BlockSpec((bm, k), lambda i, j: (i, 0)),
            pl.BlockSpec((k, bn), lambda i, j: (0, j)),
            pl.BlockSpec((k, bn), lambda i, j: (0, n_j + j)),
            pl.BlockSpec((k, bn), lambda i, j: (0, 2 * n_j + j)),
            pl.BlockSpec((_N_META, k), lambda i, j: (0, 0)),
            pl.BlockSpec((_CONV_WIDTH, bn), lambda i, j: (0, j)),
        ] + j_in,
        out_specs=[pl.BlockSpec((bm, bn), lambda i, j: (i, j))] + j_out,
        out_shape=[jax.ShapeDtypeStruct((m, _D), _BF16)] + j_shapes,
        scratch_shapes=[pltpu.VMEM((n_j, _SUBLANES, bn), _F32), pltpu.VMEM((n_j, _SUBLANES, bn), _F32)],
        compiler_params=_params(blocks + j_blocks, scratch_bytes=8 * bm * bn * 4),
        name="conv_in",
    )(a, w, w, w, a_meta, conv_w, *[j.src for j in jobs])
    return outs[0], list(outs[1:])


def kernel(x, meta, norm_mix, norm_ffn, gla_w_in, gla_w_a2, gla_b_a, gla_head_norm, gla_w_out,
           conv_w_in, conv_w, conv_w_out, ffn_w_gate, ffn_w_up, ffn_w_down, norm_final):
    n_batch, seq, d = x.shape
    xm = x.reshape(n_batch * seq, d)
    xmeta = meta.astype(x.dtype)
    n_qkvr = 2 * _QK + 2 * _V
    d_ff = ffn_w_gate.shape[2]
    assert d_ff % 256 == 0

    w_qkvr = _cast_cols(gla_w_in, 0, n_qkvr, rb=512, cb=1024)
    w_a1 = jnp.pad(gla_w_in[0, :, n_qkvr:], ((0, 0), (0, _LANES - _RANK))).astype(_BF16)
    w_a2 = jnp.pad(gla_w_a2[0], ((0, _LANES - _RANK), (0, 0))).astype(_BF16)

    hn = _rmsnorm(xm, norm_mix[0], _BF16)
    hn_meta = _rmsnorm(xmeta, norm_mix[0], _BF16)
    proj, proj_meta, (w_out0, w_gate0) = _matmul(
        hn, [(w_qkvr, 0)], n_qkvr, _epilogue_cast, _BF16, bm=1024, bn=1024, meta=(hn_meta, None),
        jobs=[_CastJob(gla_w_out, 0, 64), _CastJob(ffn_w_gate, 0, 128)], name="gla_in")
    a_low, a_low_meta, _ = _matmul(hn, [(w_a1, 0)], _LANES, _epilogue_cast, _F32, bm=1024, bn=_LANES,
                                   meta=(hn_meta, None), name="gla_gate_in")

    zero_state = jnp.zeros((_HEADS, _DK, _DV), _F32)
    o_meta, s_meta, _ = _gla(proj_meta, a_low_meta, w_a2, gla_b_a[0], gla_head_norm[0], zero_state,
                             n_batch=1, seq=_N_META, chunk=_N_META, rows_per_step=_N_META)
    o_x, _, (w_up0, w_down0, w_cin) = _gla(
        proj, a_low, w_a2, gla_b_a[0], gla_head_norm[0], s_meta[0],
        n_batch=n_batch, seq=seq, chunk=64, rows_per_step=512,
        jobs=[_CastJob(ffn_w_up, 0, 128), _CastJob(ffn_w_down, 0, d_ff // 128), _CastJob(conv_w_in, 0, 128)])
    h, h_meta, _ = _matmul(
        o_x, [(w_out0, 0)], d, _epilogue_resid, _F32, bm=1024, bn=1024, resid=xm, meta=(o_meta, xmeta),
        name="gla_out")

    hn = _rmsnorm(h, norm_ffn[0], _BF16)
    hn_meta = _rmsnorm(h_meta, norm_ffn[0], _BF16)
    act, act_meta, (w_cout, w_gate1, w_up1) = _matmul(
        hn, [(w_gate0, 0), (w_up0, 0)], d_ff, _epilogue_swiglu, _BF16, bm=2048, bn=256, meta=(hn_meta, None),
        jobs=[_CastJob(conv_w_out, 0, 64), _CastJob(ffn_w_gate, 1, 256), _CastJob(ffn_w_up, 1, 256)],
        name="ffn_gate_up")
    h, h_meta, (w_down1,) = _matmul(
        act, [(w_down0, 0)], d, _epilogue_resid, _F32, bm=512, bn=512, resid=h, meta=(act_meta, h_meta),
        jobs=[_CastJob(ffn_w_down, 1, d_ff // 64)], name="ffn_down")

    hn = _rmsnorm(h, norm_mix[1], _BF16)
    hn_meta = _rmsnorm(h_meta, norm_mix[1], _BF16)
    y, _ = _conv_in(hn, hn_meta, w_cin, conv_w[0], seq=seq, bm=1024, bn=256)
    h, _, _ = _matmul(y, [(w_cout, 0)], d, _epilogue_resid, _F32, bm=1024, bn=1024, resid=h, name="conv_out")

    hn = _rmsnorm(h, norm_ffn[1], _BF16)
    act, _, _ = _matmul(hn, [(w_gate1, 0), (w_up1, 0)], d_ff, _epilogue_swiglu, _BF16, bm=2048, bn=256,
                        name="ffn_gate_up")
    h, _, _ = _matmul(act, [(w_down1, 0)], d, _epilogue_resid, _F32, bm=512, bn=512, resid=h, name="ffn_down")
    out = _rmsnorm(h, norm_final, _F32)
    return out.reshape(n_batch, seq, d)
```

```python
import functools
from typing import NamedTuple

import jax
import jax.numpy as jnp
from jax import lax
from jax.experimental import pallas as pl
from jax.experimental.pallas import tpu as pltpu

_D = 4096
_N_META = 16
_HEADS = 4
_DK = 512
_DV = 1024
_QK = _HEADS * _DK
_V = _HEADS * _DV
_RANK = 16
_GATE_TAU = 16.0
_CONV_WIDTH = 3
_EPS = 1e-6
_LOG2E = 1.4426950408889634

_LANES = 128
_SUBLANES = 8
_BF16_SUBLANES = 16
_V7X_SCOPED_VMEM_BYTES = 60000 * 1024

_F32 = jnp.float32
_BF16 = jnp.bfloat16


def _params(block_bytes, scratch_bytes=0, n_grid=2):
    need = 2 * sum(block_bytes) + scratch_bytes
    assert need <= _V7X_SCOPED_VMEM_BYTES, (need, _V7X_SCOPED_VMEM_BYTES)
    return pltpu.CompilerParams(
        dimension_semantics=("arbitrary",) * n_grid,
        vmem_limit_bytes=_V7X_SCOPED_VMEM_BYTES,
    )


class _CastJob(NamedTuple):
    src: jax.Array
    layer: int
    n_blocks: int
    tile_cols: int
    gain: jax.Array = None

    @property
    def block(self):
        rows, cols = self.src.shape[1:]
        assert rows % self.n_blocks == 0 and (rows // self.n_blocks) % _BF16_SUBLANES == 0
        assert cols % self.tile_cols == 0
        return rows // self.n_blocks, cols


def _job_plumbing(jobs, n_steps, flat_step):
    in_specs, args, out_specs, out_shapes, blocks, desc = [], [], [], [], [], []
    for job in jobs:
        assert job.n_blocks <= n_steps, (job.n_blocks, n_steps)
        rb, cols = job.block
        last, layer = job.n_blocks - 1, job.layer

        def blk(*g, last=last):
            return jnp.minimum(flat_step(*g), last)

        in_specs.append(pl.BlockSpec((1, rb, cols), lambda *g, blk=blk, layer=layer: (layer, blk(*g), 0)))
        args.append(job.src)
        if job.gain is not None:
            in_specs.append(pl.BlockSpec((rb, 1), lambda *g, blk=blk: (blk(*g), 0)))
            args.append(job.gain.reshape(-1, 1))
        n_tiles, tc = cols // job.tile_cols, job.tile_cols
        out_specs.append(pl.BlockSpec((n_tiles, rb, tc), lambda *g, blk=blk: (0, blk(*g), 0)))
        out_shapes.append(jax.ShapeDtypeStruct((n_tiles, job.src.shape[1], tc), _BF16))
        blocks += [rb * cols * 4, rb * cols * 2, rb * _LANES * 4]
        desc.append((job.n_blocks, job.gain is not None))
    return in_specs, args, out_specs, out_shapes, blocks, tuple(desc)


def _take_job_inputs(it, desc):
    return [(next(it), next(it) if has_gain else None) for _, has_gain in desc]


def _run_casts(step, desc, in_refs, dst_refs):
    for (n_blocks, _), (src, gain), dst in zip(desc, in_refs, dst_refs):
        @pl.when(step < n_blocks)
        def _cast(src=src, gain=gain, dst=dst):
            w = src[0]
            if gain is not None:
                w = w * gain[...]
            w = w.astype(_BF16)
            n_tiles, _, tc = dst.shape
            for t in range(n_tiles):
                dst[t] = w[:, t * tc:(t + 1) * tc]


def _cast_cols_kernel(src_ref, gain_ref, dst_ref, *, n_valid):
    w = src_ref[0] * gain_ref[...]
    if n_valid < w.shape[1]:
        w = jnp.where(lax.broadcasted_iota(jnp.int32, w.shape, 1) < n_valid, w, 0.0)
    dst_ref[0] = w.astype(_BF16)


def _cast_cols(src, layer, gain, col0, n_cols, rb, cb):
    rows, src_cols = src.shape[1:]
    n_valid = min(n_cols, src_cols - col0)
    n_cols = -(-n_cols // cb) * cb
    assert rows % rb == 0 and col0 % cb == 0 and (n_valid == n_cols or n_cols == cb)
    c0 = col0 // cb
    return pl.pallas_call(
        functools.partial(_cast_cols_kernel, n_valid=n_valid),
        grid=(rows // rb, n_cols // cb),
        in_specs=[pl.BlockSpec((1, rb, cb), lambda i, j: (layer, i, c0 + j)),
                  pl.BlockSpec((rb, 1), lambda i, j: (i, 0))],
        out_specs=pl.BlockSpec((1, rb, cb), lambda i, j: (j, i, 0)),
        out_shape=jax.ShapeDtypeStruct((n_cols // cb, rows, cb), _BF16),
        compiler_params=_params([rb * cb * 4, rb * cb * 2, rb * _LANES * 4]),
        name="weight_cast",
    )(src, gain.reshape(-1, 1))


def _row_stats_kernel(x_ref, xb_ref, rstd_ref):
    x = x_ref[...]
    xb_ref[...] = x.astype(_BF16)
    rstd_ref[...] = lax.rsqrt(jnp.mean(x * x, axis=-1, keepdims=True) + _EPS)


def _row_stats(x, bm=256):
    m, d = x.shape
    bm = min(bm, m)
    assert m % bm == 0
    return pl.pallas_call(
        _row_stats_kernel,
        grid=(m // bm,),
        in_specs=[pl.BlockSpec((bm, d), lambda i: (i, 0))],
        out_specs=[pl.BlockSpec((bm, d), lambda i: (i, 0)), pl.BlockSpec((bm, 1), lambda i: (i, 0))],
        out_shape=[jax.ShapeDtypeStruct((m, d), _BF16), jax.ShapeDtypeStruct((m, 1), _F32)],
        compiler_params=_params([bm * d * 4, bm * d * 2, bm * _LANES * 4], scratch_bytes=2 * bm * d * 4, n_grid=1),
        name="row_stats",
    )(x)


def _rmsnorm_kernel(x_ref, g_ref, o_ref):
    x = x_ref[...]
    ms = jnp.mean(x * x, axis=-1, keepdims=True)
    o_ref[...] = (x * lax.rsqrt(ms + _EPS) * g_ref[...]).astype(o_ref.dtype)


def _rmsnorm(x, g, out_dtype, bm=256):
    m, d = x.shape
    bm = min(bm, m)
    assert m % bm == 0
    blocks = [bm * d * 4, d * 4, bm * d * jnp.dtype(out_dtype).itemsize]
    return pl.pallas_call(
        _rmsnorm_kernel,
        grid=(m // bm,),
        in_specs=[pl.BlockSpec((bm, d), lambda i: (i, 0)), pl.BlockSpec((1, d), lambda i: (0, 0))],
        out_specs=pl.BlockSpec((bm, d), lambda i: (i, 0)),
        out_shape=jax.ShapeDtypeStruct((m, d), out_dtype),
        compiler_params=_params(blocks, scratch_bytes=2 * bm * d * 4, n_grid=1),
        name="rmsnorm",
    )(x, g.reshape(1, d))


def _epilogue_cast(accs, resid):
    return accs[0]


def _epilogue_resid(accs, resid):
    return resid + accs[0]


def _epilogue_swiglu(accs, resid):
    gate, up = accs
    return gate * jax.nn.sigmoid(gate) * up


def _matmul_kernel(*refs, n_w, epilogue, has_resid, has_scale, has_meta, emit_stats, jobs, n_inner, n_cols):
    it = iter(refs)
    a_ref = next(it)
    w_refs = [next(it) for _ in range(n_w)]
    r_ref = next(it) if has_resid else None
    s_ref = next(it) if has_scale else None
    am_ref = next(it) if has_meta else None
    rm_ref = next(it) if has_meta and has_resid else None
    sm_ref = next(it) if has_meta and has_scale else None
    job_in = _take_job_inputs(it, jobs)
    o_ref = next(it)
    ob_ref, rstd_ref = (next(it), next(it)) if emit_stats else (None, None)
    om_ref = next(it) if has_meta else None
    dst_refs = [next(it) for _ in jobs]
    ssq_scr = next(it) if emit_stats else None
    i, j = pl.program_id(0), pl.program_id(1)

    def result(a, resid, scale):
        accs = [jnp.dot(a, w[0], preferred_element_type=_F32) for w in w_refs]
        if scale is not None:
            accs = [acc * scale for acc in accs]
        return epilogue(accs, resid)

    out = result(a_ref[...], r_ref[...] if has_resid else None, s_ref[...] if has_scale else None)
    o_ref[...] = out.astype(o_ref.dtype)

    if emit_stats:
        ob_ref[...] = out.astype(_BF16)
        part = jnp.sum(out * out, axis=-1, keepdims=True)

        @pl.when(j == 0)
        def _first():
            ssq_scr[...] = part

        @pl.when(j != 0)
        def _accumulate():
            ssq_scr[...] += part

        @pl.when(j == n_inner - 1)
        def _finish():
            rstd_ref[...] = lax.rsqrt(ssq_scr[...] * (1.0 / n_cols) + _EPS)

    if has_meta:
        @pl.when(i == 0)
        def _meta_rows():
            out_m = result(am_ref[...], rm_ref[...] if has_resid else None, sm_ref[...] if has_scale else None)
            om_ref[0] = out_m.astype(om_ref.dtype)

        @pl.when(i != 0)
        def _meta_unused():
            om_ref[0] = jnp.zeros(om_ref.shape[1:], om_ref.dtype)

    _run_casts(i * n_inner + j, jobs, job_in, dst_refs)


def _matmul(a, ws, n_out, epilogue, out_dtype, bm, bn, resid=None, scale=None, meta=None, emit_stats=False,
            jobs=(), name="matmul"):
    m, k = a.shape
    bm = min(bm, m)
    assert m % bm == 0 and n_out % bn == 0
    n_i, n_j = m // bm, n_out // bn
    out_bytes = jnp.dtype(out_dtype).itemsize
    has_resid, has_scale, has_meta = resid is not None, scale is not None, meta is not None

    in_specs = [pl.BlockSpec((bm, k), lambda i, j: (i, 0))]
    args = [a]
    blocks = [bm * k * 2, bm * bn * out_bytes]
    for w, off in ws:
        assert w.shape[1:] == (k, bn), (w.shape, k, bn)
        in_specs.append(pl.BlockSpec((1, k, bn), lambda i, j, off=off: (off + j, 0, 0)))
        args.append(w)
        blocks.append(k * bn * 2)
    if has_resid:
        in_specs.append(pl.BlockSpec((bm, bn), lambda i, j: (i, j)))
        args.append(resid)
        blocks.append(bm * bn * 4)
    if has_scale:
        in_specs.append(pl.BlockSpec((bm, 1), lambda i, j: (i, 0)))
        args.append(scale[0])
        blocks.append(bm * _LANES * 4)
    out_specs = [pl.BlockSpec((bm, bn), lambda i, j: (i, j))]
    out_shapes = [jax.ShapeDtypeStruct((m, n_out), out_dtype)]
    scratch = []
    if emit_stats:
        out_specs += [pl.BlockSpec((bm, bn), lambda i, j: (i, j)), pl.BlockSpec((bm, 1), lambda i, j: (i, 0))]
        out_shapes += [jax.ShapeDtypeStruct((m, n_out), _BF16), jax.ShapeDtypeStruct((m, 1), _F32)]
        blocks += [bm * bn * 2, bm * _LANES * 4]
        scratch.append(pltpu.VMEM((bm, 1), _F32))
    if has_meta:
        a_meta, r_meta = meta
        in_specs.append(pl.BlockSpec((_N_META, k), lambda i, j: (0, 0)))
        args.append(a_meta)
        blocks.append(_N_META * k * 2)
        if has_resid:
            in_specs.append(pl.BlockSpec((_N_META, bn), lambda i, j: (0, j)))
            args.append(r_meta)
        if has_scale:
            in_specs.append(pl.BlockSpec((_N_META, 1), lambda i, j: (0, 0)))
            args.append(scale[1])
        out_specs.append(pl.BlockSpec((1, _N_META, bn), lambda i, j: (i, 0, j)))
        out_shapes.append(jax.ShapeDtypeStruct((n_i, _N_META, n_out), out_dtype))
    j_in, j_args, j_out, j_shapes, j_blocks, j_desc = _job_plumbing(jobs, n_i * n_j, lambda i, j: i * n_j + j)

    kernel = functools.partial(_matmul_kernel, n_w=len(ws), epilogue=epilogue, has_resid=has_resid,
                               has_scale=has_scale, has_meta=has_meta, emit_stats=emit_stats, jobs=j_desc,
                               n_inner=n_j, n_cols=n_out)
    outs = pl.pallas_call(
        kernel,
        grid=(n_i, n_j),
        in_specs=in_specs + j_in,
        out_specs=out_specs + j_out,
        out_shape=out_shapes + j_shapes,
        scratch_shapes=scratch,
        compiler_params=_params(blocks + j_blocks, scratch_bytes=(len(ws) + 1 + emit_stats) * bm * bn * 4),
        name=name,
    )(*args, *j_args)
    n_main = 3 if emit_stats else 1
    main = list(outs[:n_main])
    out_meta = outs[n_main][0] if has_meta else None
    return (*main, out_meta, list(outs[n_main + has_meta:]))


def _cumsum_rows(tri, g):
    g1 = g.astype(_BF16)
    r1 = g - g1.astype(_F32)
    g2 = r1.astype(_BF16)
    g3 = (r1 - g2.astype(_F32)).astype(_BF16)
    out = jnp.dot(tri, g1, preferred_element_type=_F32)
    out = out + jnp.dot(tri, g2, preferred_element_type=_F32)
    return out + jnp.dot(tri, g3, preferred_element_type=_F32)


def _gla_kernel(*refs, chunk, n_chunks, sub, jobs, n_steps):
    it = iter(refs)
    q_ref, k_ref, v_ref, r_ref, al_ref, wa2_ref, ba_ref, hnorm_ref, s0_ref = (next(it) for _ in range(9))
    job_in = _take_job_inputs(it, jobs)
    o_ref, sfin_ref = next(it), next(it)
    dst_refs = [next(it) for _ in jobs]
    s_scr, b_scr, bnext_scr, o_scr = next(it), next(it), next(it), next(it)
    step = pl.program_id(2)

    @pl.when(step == 0)
    def _load_state():
        s_scr[...] = s0_ref[0]

    c = chunk
    state_tile = 2 * _LANES
    n_sub = c // sub
    tri = (lax.broadcasted_iota(jnp.int32, (c, c), 0) >= lax.broadcasted_iota(jnp.int32, (c, c), 1)).astype(_BF16)
    sub_row = lax.broadcasted_iota(jnp.int32, (sub, c), 0)
    sub_col = lax.broadcasted_iota(jnp.int32, (sub, c), 1)

    q_scale = _DK ** -0.5
    trans_b = (((1,), (1,)), ((), ()))

    def gate_of_chunk(ci):
        rows = pl.ds(ci * c if isinstance(ci, int) else pl.multiple_of(ci * c, c), c)
        return jnp.dot(al_ref[rows, :].astype(_BF16), wa2_ref[...], preferred_element_type=_F32) + ba_ref[...]

    def decay_of_gate(x):
        g = (jnp.minimum(x, 0.0) - jnp.log(1.0 + jnp.exp(-jnp.abs(x)))) * (_LOG2E / _GATE_TAU)
        return _cumsum_rows(tri, g)

    def one_chunk(ci, carry):
        base = pl.multiple_of(ci * c, c)
        rows = pl.ds(base, c)
        b_scr[...] = bnext_scr[...]
        b_last = b_scr[c - 1:c, :]
        x_next = gate_of_chunk(jnp.minimum(ci + 1, n_chunks - 1))

        q_dec = (q_ref[rows, :].astype(_F32) * q_scale * jnp.exp2(b_scr[...])).astype(_BF16)
        o_scr[...] = jnp.dot(q_dec, s_scr[...].astype(_BF16), preferred_element_type=_F32)

        diag, off = [], [None]
        for blk in range(n_sub):
            r0 = blk * sub
            b_b = b_scr[r0:r0 + sub, :]
            q_b = q_ref[pl.ds(base + r0, sub), :].astype(_F32) * q_scale
            stacked = jnp.concatenate(
                [q_b * jnp.exp2(b_b - b_scr[r0 + j:r0 + j + 1, :]) for j in range(sub)], axis=0).astype(_BF16)
            diag.append(lax.dot_general(stacked, k_ref[rows, :], trans_b, preferred_element_type=_F32))
            if blk > 0:
                p = b_scr[r0 - 1:r0, :]
                q_o = (q_b * jnp.exp2(b_b - p)).astype(_BF16)
                k_o = (k_ref[rows, :].astype(_F32) * jnp.exp2(p - b_scr[...])).astype(_BF16)
                off.append(lax.dot_general(q_o, k_o, trans_b, preferred_element_type=_F32))

        b_next = decay_of_gate(x_next)

        tiles = []
        for blk in range(n_sub):
            r0 = blk * sub
            sc = jnp.zeros((sub, c), _F32)
            for j in range(sub):
                sc = jnp.where(sub_col == r0 + j, diag[blk][j * sub:(j + 1) * sub], sc)
            sc = jnp.where(sub_row + r0 >= sub_col, sc, 0.0)
            if blk > 0:
                sc = sc + jnp.where(sub_col < r0, off[blk], 0.0)
            tiles.append(sc)
        scores = jnp.concatenate(tiles, axis=0) if n_sub > 1 else tiles[0]
        o_intra = jnp.dot(scores.astype(_BF16), v_ref[rows, :], preferred_element_type=_F32)
        bnext_scr[...] = b_next

        k_dec = (k_ref[rows, :].astype(_F32) * jnp.exp2(b_last - b_scr[...])).astype(_BF16)
        decay_col = jnp.transpose(jnp.broadcast_to(jnp.exp2(b_last), (_LANES, _DK)))
        for t in range(_DV // state_tile):
            cols = slice(t * state_tile, (t + 1) * state_tile)
            upd = lax.dot_general(k_dec, v_ref[rows, cols], (((0,), (0,)), ((), ())), preferred_element_type=_F32)
            s_scr[:, cols] = s_scr[:, cols] * jnp.tile(decay_col, (1, state_tile // _LANES)) + upd

        o = o_scr[...] + o_intra
        ms = jnp.mean(o * o, axis=-1, keepdims=True)
        o_n = o * lax.rsqrt(ms + _EPS) * hnorm_ref[...]
        r = r_ref[rows, :].astype(_F32)
        o_ref[rows, :] = (o_n * (r * jax.nn.sigmoid(r))).astype(o_ref.dtype)
        return carry

    bnext_scr[...] = decay_of_gate(gate_of_chunk(0))
    lax.fori_loop(0, n_chunks, one_chunk, 0)

    @pl.when(step == pl.num_programs(2) - 1)
    def _store_state():
        sfin_ref[0, 0] = s_scr[...]

    flat = (pl.program_id(0) * _HEADS + pl.program_id(1)) * n_steps + step
    _run_casts(flat, jobs, job_in, dst_refs)


def _gla(proj, a_low, wa2, ba, hnorm, s0, *, n_batch, seq, chunk, rows_per_step, jobs=()):
    m = proj.shape[0]
    assert m == n_batch * seq and seq % rows_per_step == 0 and rows_per_step % chunk == 0
    steps = seq // rows_per_step
    r = rows_per_step
    kv = _QK // _DV

    def row(b, h, i):
        return b * steps + i

    j_in, j_args, j_out, j_shapes, j_blocks, j_desc = _job_plumbing(
        jobs, n_batch * _HEADS * steps, lambda b, h, i: (b * _HEADS + h) * steps + i)
    kernel = functools.partial(_gla_kernel, chunk=chunk, n_chunks=r // chunk, sub=min(chunk, 8),
                               jobs=j_desc, n_steps=steps)
    blocks = [r * _DK * 2, r * _DK * 2, r * _DV * 2, r * _DV * 2, r * _LANES * 4, _LANES * _DK * 2,
              _DK * 4, _DV * 4, _DK * _DV * 4, r * _DV * 2, _DK * _DV * 4]
    outs = pl.pallas_call(
        kernel,
        grid=(n_batch, _HEADS, steps),
        in_specs=[
            pl.BlockSpec((r, _DK), lambda b, h, i: (row(b, h, i), h)),
            pl.BlockSpec((r, _DK), lambda b, h, i: (row(b, h, i), _HEADS + h)),
            pl.BlockSpec((r, _DV), lambda b, h, i: (row(b, h, i), 2 * kv + h)),
            pl.BlockSpec((r, _DV), lambda b, h, i: (row(b, h, i), 2 * kv + _HEADS + h)),
            pl.BlockSpec((r, _LANES), lambda b, h, i: (row(b, h, i), 0)),
            pl.BlockSpec((_LANES, _DK), lambda b, h, i: (0, h)),
            pl.BlockSpec((1, _DK), lambda b, h, i: (0, h)),
            pl.BlockSpec((1, _DV), lambda b, h, i: (0, 0)),
            pl.BlockSpec((1, _DK, _DV), lambda b, h, i: (h, 0, 0)),
        ] + j_in,
        out_specs=[
            pl.BlockSpec((r, _DV), lambda b, h, i: (row(b, h, i), h)),
            pl.BlockSpec((1, 1, _DK, _DV), lambda b, h, i: (b, h, 0, 0)),
        ] + j_out,
        out_shape=[
            jax.ShapeDtypeStruct((m, _V), _BF16),
            jax.ShapeDtypeStruct((n_batch, _HEADS, _DK, _DV), _F32),
        ] + j_shapes,
        scratch_shapes=[pltpu.VMEM((_DK, _DV), _F32), pltpu.VMEM((chunk, _DK), _F32), pltpu.VMEM((chunk, _DK), _F32),
                        pltpu.VMEM((chunk, _DV), _F32)],
        compiler_params=_params(blocks + j_blocks, scratch_bytes=4 * _DK * _DV * 4, n_grid=3),
        name="gla",
    )(proj, proj, proj, proj, a_low, wa2, ba.reshape(1, _QK), hnorm.reshape(1, _DV), s0, *j_args)
    return outs[0], outs[1], list(outs[2:])


def _conv_in_kernel(a_ref, wb_ref, wc_ref, wu_ref, s_ref, am_ref, sm_ref, cw_ref, o_ref, tail_scr, meta_scr,
                    *, blocks_per_seq):
    i, j = pl.program_id(0), pl.program_id(1)

    @pl.when(i == 0)
    def _meta_rows():
        am, sm = am_ref[...], sm_ref[...]
        z_meta = ((jnp.dot(am, wc_ref[0], preferred_element_type=_F32) * sm)
                  * (jnp.dot(am, wu_ref[0], preferred_element_type=_F32) * sm))
        meta_scr[j] = z_meta[_N_META - _SUBLANES:]
        tail_scr[j] = z_meta[_N_META - _SUBLANES:]

    a, s = a_ref[...], s_ref[...]
    bg = jnp.dot(a, wb_ref[0], preferred_element_type=_F32) * s
    z = ((jnp.dot(a, wc_ref[0], preferred_element_type=_F32) * s)
         * (jnp.dot(a, wu_ref[0], preferred_element_type=_F32) * s))
    prev = jnp.where(i % blocks_per_seq == 0, meta_scr[j], tail_scr[j])
    tail_scr[j] = z[z.shape[0] - _SUBLANES:]
    z_m1, z_m2 = prev[_SUBLANES - 1:_SUBLANES], prev[_SUBLANES - 2:_SUBLANES - 1]
    rid = lax.broadcasted_iota(jnp.int32, z.shape, 0)
    z1 = jnp.where(rid == 0, z_m1, pltpu.roll(z, 1, 0))
    z2 = jnp.where(rid == 0, z_m2, jnp.where(rid == 1, z_m1, pltpu.roll(z, 2, 0)))
    cw = cw_ref[...]
    conv = cw[2:3] * z + cw[0:1] * z2 + cw[1:2] * z1
    o_ref[...] = (bg * conv).astype(o_ref.dtype)


def _conv_in(a, rstd, a_meta, rstd_meta, w, conv_w, *, seq, bm, bn):
    m, k = a.shape
    assert m % bm == 0 and seq % bm == 0 and _D % bn == 0
    n_i, n_j = m // bm, _D // bn
    kernel = functools.partial(_conv_in_kernel, blocks_per_seq=seq // bm)
    blocks = [bm * k * 2, 3 * k * bn * 2, bm * _LANES * 4, _N_META * k * 2, _CONV_WIDTH * bn * 4, bm * bn * 2]
    return pl.pallas_call(
        kernel,
        grid=(n_i, n_j),
        in_specs=[
            pl.BlockSpec((bm, k), lambda i, j: (i, 0)),
            pl.BlockSpec((1, k, bn), lambda i, j: (j, 0, 0)),
            pl.BlockSpec((1, k, bn), lambda i, j: (n_j + j, 0, 0)),
            pl.BlockSpec((1, k, bn), lambda i, j: (2 * n_j + j, 0, 0)),
            pl.BlockSpec((bm, 1), lambda i, j: (i, 0)),
            pl.BlockSpec((_N_META, k), lambda i, j: (0, 0)),
            pl.BlockSpec((_N_META, 1), lambda i, j: (0, 0)),
            pl.BlockSpec((_CONV_WIDTH, bn), lambda i, j: (0, j)),
        ],
        out_specs=pl.BlockSpec((bm, bn), lambda i, j: (i, j)),
        out_shape=jax.ShapeDtypeStruct((m, _D), _BF16),
        scratch_shapes=[pltpu.VMEM((n_j, _SUBLANES, bn), _F32), pltpu.VMEM((n_j, _SUBLANES, bn), _F32)],
        compiler_params=_params(blocks, scratch_bytes=8 * bm * bn * 4),
        name="conv_in",
    )(a, w, w, w, rstd, a_meta, rstd_meta, conv_w)


def kernel(x, meta, norm_mix, norm_ffn, gla_w_in, gla_w_a2, gla_b_a, gla_head_norm, gla_w_out,
           conv_w_in, conv_w, conv_w_out, ffn_w_gate, ffn_w_up, ffn_w_down, norm_final):
    n_batch, seq, d = x.shape
    xm = x.reshape(n_batch * seq, d)
    xmeta = meta.astype(x.dtype)
    n_qkvr = 2 * _QK + 2 * _V
    d_ff = ffn_w_gate.shape[2]
    assert d_ff % 256 == 0


    w_qkvr = _cast_cols(gla_w_in, 0, norm_mix[0], 0, n_qkvr, rb=512, cb=1024)
    w_a1 = _cast_cols(gla_w_in, 0, norm_mix[0], n_qkvr, _RANK, rb=1024, cb=_LANES)
    w_a2 = jnp.pad(gla_w_a2[0], ((0, _LANES - _RANK), (0, 0))).astype(_BF16)

    hb, rstd = _row_stats(xm)
    hb_meta, rstd_meta = _row_stats(xmeta)
    proj, proj_meta, (w_out0, w_gate0) = _matmul(
        hb, [(w_qkvr, 0)], n_qkvr, _epilogue_cast, _BF16, bm=1024, bn=1024, scale=(rstd, rstd_meta),
        meta=(hb_meta, None),
        jobs=[_CastJob(gla_w_out, 0, 64, 512), _CastJob(ffn_w_gate, 0, 128, 256, norm_ffn[0])], name="gla_in")
    a_low, a_low_meta, _ = _matmul(hb, [(w_a1, 0)], _LANES, _epilogue_cast, _F32, bm=1024, bn=_LANES,
                                   scale=(rstd, rstd_meta), meta=(hb_meta, None), name="gla_gate_in")

    zero_state = jnp.zeros((_HEADS, _DK, _DV), _F32)
    o_meta, s_meta, _ = _gla(proj_meta, a_low_meta, w_a2, gla_b_a[0], gla_head_norm[0], zero_state,
                             n_batch=1, seq=_N_META, chunk=_N_META, rows_per_step=_N_META)
    o_x, _, (w_up0, w_down0, w_cin) = _gla(
        proj, a_low, w_a2, gla_b_a[0], gla_head_norm[0], s_meta[0],
        n_batch=n_batch, seq=seq, chunk=64, rows_per_step=512,
        jobs=[_CastJob(ffn_w_up, 0, 128, 256, norm_ffn[0]), _CastJob(ffn_w_down, 0, d_ff // 128, 512),
              _CastJob(conv_w_in, 0, 128, 256, norm_mix[1])])
    h, hb, rstd, h_meta, _ = _matmul(
        o_x, [(w_out0, 0)], d, _epilogue_resid, _F32, bm=1024, bn=512, resid=xm, meta=(o_meta, xmeta),
        emit_stats=True, name="gla_out")

    hb_meta, rstd_meta = _row_stats(h_meta)
    act, act_meta, (w_cout, w_gate1, w_up1) = _matmul(
        hb, [(w_gate0, 0), (w_up0, 0)], d_ff, _epilogue_swiglu, _BF16, bm=2048, bn=256, scale=(rstd, rstd_meta),
        meta=(hb_meta, None),
        jobs=[_CastJob(conv_w_out, 0, 64, 512), _CastJob(ffn_w_gate, 1, 256, 256, norm_ffn[1]),
              _CastJob(ffn_w_up, 1, 256, 256, norm_ffn[1])],
        name="ffn_gate_up")
    h, hb, rstd, h_meta, (w_down1,) = _matmul(
        act, [(w_down0, 0)], d, _epilogue_resid, _F32, bm=512, bn=512, resid=h, meta=(act_meta, h_meta),
        emit_stats=True, jobs=[_CastJob(ffn_w_down, 1, d_ff // 64, 512)], name="ffn_down")

    hb_meta, rstd_meta = _row_stats(h_meta)
    y = _conv_in(hb, rstd, hb_meta, rstd_meta, w_cin, conv_w[0], seq=seq, bm=1024, bn=256)
    h, hb, rstd, _, _ = _matmul(y, [(w_cout, 0)], d, _epilogue_resid, _F32, bm=1024, bn=512, resid=h,
                                emit_stats=True, name="conv_out")

    act, _, _ = _matmul(hb, [(w_gate1, 0), (w_up1, 0)], d_ff, _epilogue_swiglu, _BF16, bm=2048, bn=256,
                        scale=(rstd, None), name="ffn_gate_up")
    h, _, _ = _matmul(act, [(w_down1, 0)], d, _epilogue_resid, _F32, bm=512, bn=512, resid=h, name="ffn_down")
    out = _rmsnorm(h, norm_final, _F32)
    return out.reshape(n_batch, seq, d)
```

```python
import functools
from typing import NamedTuple

import jax
import jax.numpy as jnp
from jax import lax
from jax.experimental import pallas as pl
from jax.experimental.pallas import tpu as pltpu

_D = 4096
_N_META = 16
_HEADS = 4
_DK = 512
_DV = 1024
_QK = _HEADS * _DK
_V = _HEADS * _DV
_RANK = 16
_GATE_TAU = 16.0
_CONV_WIDTH = 3
_EPS = 1e-6
_LOG2E = 1.4426950408889634

_LANES = 128
_SUBLANES = 8
_BF16_SUBLANES = 16
_V7X_SCOPED_VMEM_BYTES = 60000 * 1024

_F32 = jnp.float32
_BF16 = jnp.bfloat16


def _params(block_bytes, scratch_bytes=0, n_grid=2):
    need = 2 * sum(block_bytes) + scratch_bytes
    assert need <= _V7X_SCOPED_VMEM_BYTES, (need, _V7X_SCOPED_VMEM_BYTES)
    return pltpu.CompilerParams(
        dimension_semantics=("arbitrary",) * n_grid,
        vmem_limit_bytes=_V7X_SCOPED_VMEM_BYTES,
    )


class _CastJob(NamedTuple):
    src: jax.Array
    layer: int
    n_blocks: int
    tile_cols: int
    gain: jax.Array = None

    @property
    def block(self):
        rows, cols = self.src.shape[1:]
        assert rows % self.n_blocks == 0 and (rows // self.n_blocks) % _BF16_SUBLANES == 0
        assert cols % self.tile_cols == 0
        return rows // self.n_blocks, cols


def _job_plumbing(jobs, n_steps, flat_step):
    in_specs, args, out_specs, out_shapes, blocks, desc = [], [], [], [], [], []
    for job in jobs:
        assert job.n_blocks <= n_steps, (job.n_blocks, n_steps)
        rb, cols = job.block
        last, layer = job.n_blocks - 1, job.layer

        def blk(*g, last=last):
            return jnp.minimum(flat_step(*g), last)

        in_specs.append(pl.BlockSpec((1, rb, cols), lambda *g, blk=blk, layer=layer: (layer, blk(*g), 0)))
        args.append(job.src)
        if job.gain is not None:
            in_specs.append(pl.BlockSpec((rb, 1), lambda *g, blk=blk: (blk(*g), 0)))
            args.append(job.gain.reshape(-1, 1))
        n_tiles, tc = cols // job.tile_cols, job.tile_cols
        out_specs.append(pl.BlockSpec((n_tiles, rb, tc), lambda *g, blk=blk: (0, blk(*g), 0)))
        out_shapes.append(jax.ShapeDtypeStruct((n_tiles, job.src.shape[1], tc), _BF16))
        blocks += [rb * cols * 4, rb * cols * 2, rb * _LANES * 4]
        desc.append(job.gain is not None)
    return in_specs, args, out_specs, out_shapes, blocks, tuple(desc)


def _take_job_inputs(it, desc):
    return [(next(it), next(it) if has_gain else None) for has_gain in desc]


def _run_casts(in_refs, dst_refs):
    for (src, gain), dst in zip(in_refs, dst_refs):
        w = src[0]
        if gain is not None:
            w = w * gain[...]
        w = w.astype(_BF16)
        n_tiles, _, tc = dst.shape
        for t in range(n_tiles):
            dst[t] = w[:, t * tc:(t + 1) * tc]


def _cast_cols_kernel(src_ref, gain_ref, dst_ref, *, n_valid):
    w = src_ref[0] * gain_ref[...]
    if n_valid < w.shape[1]:
        w = jnp.where(lax.broadcasted_iota(jnp.int32, w.shape, 1) < n_valid, w, 0.0)
    dst_ref[0] = w.astype(_BF16)


def _cast_cols(src, layer, gain, col0, n_cols, rb, cb):
    rows, src_cols = src.shape[1:]
    n_valid = min(n_cols, src_cols - col0)
    n_cols = -(-n_cols // cb) * cb
    assert rows % rb == 0 and col0 % cb == 0 and (n_valid == n_cols or n_cols == cb)
    c0 = col0 // cb
    return pl.pallas_call(
        functools.partial(_cast_cols_kernel, n_valid=n_valid),
        grid=(rows // rb, n_cols // cb),
        in_specs=[pl.BlockSpec((1, rb, cb), lambda i, j: (layer, i, c0 + j)),
                  pl.BlockSpec((rb, 1), lambda i, j: (i, 0))],
        out_specs=pl.BlockSpec((1, rb, cb), lambda i, j: (j, i, 0)),
        out_shape=jax.ShapeDtypeStruct((n_cols // cb, rows, cb), _BF16),
        compiler_params=_params([rb * cb * 4, rb * cb * 2, rb * _LANES * 4]),
        name="weight_cast",
    )(src, gain.reshape(-1, 1))


def _row_stats_kernel(x_ref, xb_ref, rstd_ref):
    x = x_ref[...]
    xb_ref[...] = x.astype(_BF16)
    rstd_ref[...] = lax.rsqrt(jnp.mean(x * x, axis=-1, keepdims=True) + _EPS)


def _row_stats(x, bm=256):
    m, d = x.shape
    bm = min(bm, m)
    assert m % bm == 0
    return pl.pallas_call(
        _row_stats_kernel,
        grid=(m // bm,),
        in_specs=[pl.BlockSpec((bm, d), lambda i: (i, 0))],
        out_specs=[pl.BlockSpec((bm, d), lambda i: (i, 0)), pl.BlockSpec((bm, 1), lambda i: (i, 0))],
        out_shape=[jax.ShapeDtypeStruct((m, d), _BF16), jax.ShapeDtypeStruct((m, 1), _F32)],
        compiler_params=_params([bm * d * 4, bm * d * 2, bm * _LANES * 4], scratch_bytes=2 * bm * d * 4, n_grid=1),
        name="row_stats",
    )(x)


def _rmsnorm_kernel(x_ref, g_ref, o_ref):
    x = x_ref[...]
    ms = jnp.mean(x * x, axis=-1, keepdims=True)
    o_ref[...] = (x * lax.rsqrt(ms + _EPS) * g_ref[...]).astype(o_ref.dtype)


def _rmsnorm(x, g, out_dtype, bm=256):
    m, d = x.shape
    bm = min(bm, m)
    assert m % bm == 0
    blocks = [bm * d * 4, d * 4, bm * d * jnp.dtype(out_dtype).itemsize]
    return pl.pallas_call(
        _rmsnorm_kernel,
        grid=(m // bm,),
        in_specs=[pl.BlockSpec((bm, d), lambda i: (i, 0)), pl.BlockSpec((1, d), lambda i: (0, 0))],
        out_specs=pl.BlockSpec((bm, d), lambda i: (i, 0)),
        out_shape=jax.ShapeDtypeStruct((m, d), out_dtype),
        compiler_params=_params(blocks, scratch_bytes=2 * bm * d * 4, n_grid=1),
        name="rmsnorm",
    )(x, g.reshape(1, d))


def _epilogue_cast(accs, resid):
    return accs[0]


def _epilogue_resid(accs, resid):
    return resid + accs[0]


def _epilogue_swiglu(accs, resid):
    gate, up = accs
    return gate * jax.nn.sigmoid(gate) * up


def _matmul_kernel(*refs, n_w, epilogue, has_resid, has_scale, has_meta, emit_stats, jobs, n_inner, n_cols):
    it = iter(refs)
    a_ref = next(it)
    w_refs = [next(it) for _ in range(n_w)]
    r_ref = next(it) if has_resid else None
    s_ref = next(it) if has_scale else None
    am_ref = next(it) if has_meta else None
    rm_ref = next(it) if has_meta and has_resid else None
    sm_ref = next(it) if has_meta and has_scale else None
    job_in = _take_job_inputs(it, jobs)
    o_ref = next(it)
    ob_ref, rstd_ref = (next(it), next(it)) if emit_stats else (None, None)
    om_ref = next(it) if has_meta else None
    dst_refs = [next(it) for _ in jobs]
    ssq_scr = next(it) if emit_stats else None
    i, j = pl.program_id(0), pl.program_id(1)

    _run_casts(job_in, dst_refs)

    def result(a, resid, scale):
        accs = [jnp.dot(a, w[0], preferred_element_type=_F32) for w in w_refs]
        if scale is not None:
            accs = [acc * scale for acc in accs]
        return epilogue(accs, resid)

    out = result(a_ref[...], r_ref[...] if has_resid else None, s_ref[...] if has_scale else None)
    o_ref[...] = out.astype(o_ref.dtype)

    if emit_stats:
        ob_ref[...] = out.astype(_BF16)
        part = jnp.sum(out * out, axis=-1, keepdims=True)

        @pl.when(j == 0)
        def _first():
            ssq_scr[...] = part

        @pl.when(j != 0)
        def _accumulate():
            ssq_scr[...] += part

        @pl.when(j == n_inner - 1)
        def _finish():
            rstd_ref[...] = lax.rsqrt(ssq_scr[...] * (1.0 / n_cols) + _EPS)

    if has_meta:
        @pl.when(i == 0)
        def _meta_rows():
            out_m = result(am_ref[...], rm_ref[...] if has_resid else None, sm_ref[...] if has_scale else None)
            om_ref[0] = out_m.astype(om_ref.dtype)

        @pl.when(i != 0)
        def _meta_unused():
            om_ref[0] = jnp.zeros(om_ref.shape[1:], om_ref.dtype)


def _matmul(a, ws, n_out, epilogue, out_dtype, bm, bn, resid=None, scale=None, meta=None, emit_stats=False,
            jobs=(), name="matmul"):
    m, k = a.shape
    bm = min(bm, m)
    assert m % bm == 0 and n_out % bn == 0
    n_i, n_j = m // bm, n_out // bn
    out_bytes = jnp.dtype(out_dtype).itemsize
    has_resid, has_scale, has_meta = resid is not None, scale is not None, meta is not None

    in_specs = [pl.BlockSpec((bm, k), lambda i, j: (i, 0))]
    args = [a]
    blocks = [bm * k * 2, bm * bn * out_bytes]
    for w, off in ws:
        assert w.shape[1:] == (k, bn), (w.shape, k, bn)
        in_specs.append(pl.BlockSpec((1, k, bn), lambda i, j, off=off: (off + j, 0, 0)))
        args.append(w)
        blocks.append(k * bn * 2)
    if has_resid:
        in_specs.append(pl.BlockSpec((bm, bn), lambda i, j: (i, j)))
        args.append(resid)
        blocks.append(bm * bn * 4)
    if has_scale:
        in_specs.append(pl.BlockSpec((bm, 1), lambda i, j: (i, 0)))
        args.append(scale[0])
        blocks.append(bm * _LANES * 4)
    out_specs = [pl.BlockSpec((bm, bn), lambda i, j: (i, j))]
    out_shapes = [jax.ShapeDtypeStruct((m, n_out), out_dtype)]
    scratch = []
    if emit_stats:
        out_specs += [pl.BlockSpec((bm, bn), lambda i, j: (i, j)), pl.BlockSpec((bm, 1), lambda i, j: (i, 0))]
        out_shapes += [jax.ShapeDtypeStruct((m, n_out), _BF16), jax.ShapeDtypeStruct((m, 1), _F32)]
        blocks += [bm * bn * 2, bm * _LANES * 4]
        scratch.append(pltpu.VMEM((bm, 1), _F32))
    if has_meta:
        a_meta, r_meta = meta
        in_specs.append(pl.BlockSpec((_N_META, k), lambda i, j: (0, 0)))
        args.append(a_meta)
        blocks.append(_N_META * k * 2)
        if has_resid:
            in_specs.append(pl.BlockSpec((_N_META, bn), lambda i, j: (0, j)))
            args.append(r_meta)
        if has_scale:
            in_specs.append(pl.BlockSpec((_N_META, 1), lambda i, j: (0, 0)))
            args.append(scale[1])
        out_specs.append(pl.BlockSpec((1, _N_META, bn), lambda i, j: (i, 0, j)))
        out_shapes.append(jax.ShapeDtypeStruct((n_i, _N_META, n_out), out_dtype))
    j_in, j_args, j_out, j_shapes, j_blocks, j_desc = _job_plumbing(jobs, n_i * n_j, lambda i, j: i * n_j + j)

    kernel = functools.partial(_matmul_kernel, n_w=len(ws), epilogue=epilogue, has_resid=has_resid,
                               has_scale=has_scale, has_meta=has_meta, emit_stats=emit_stats, jobs=j_desc,
                               n_inner=n_j, n_cols=n_out)
    outs = pl.pallas_call(
        kernel,
        grid=(n_i, n_j),
        in_specs=in_specs + j_in,
        out_specs=out_specs + j_out,
        out_shape=out_shapes + j_shapes,
        scratch_shapes=scratch,
        compiler_params=_params(blocks + j_blocks, scratch_bytes=(len(ws) + 1 + emit_stats) * bm * bn * 4),
        name=name,
    )(*args, *j_args)
    n_main = 3 if emit_stats else 1
    main = list(outs[:n_main])
    out_meta = outs[n_main][0] if has_meta else None
    return (*main, out_meta, list(outs[n_main + has_meta:]))


def _cumsum_rows(tri, g):
    g1 = g.astype(_BF16)
    r1 = g - g1.astype(_F32)
    g2 = r1.astype(_BF16)
    g3 = (r1 - g2.astype(_F32)).astype(_BF16)
    out = jnp.dot(tri, g1, preferred_element_type=_F32)
    out = out + jnp.dot(tri, g2, preferred_element_type=_F32)
    return out + jnp.dot(tri, g3, preferred_element_type=_F32)


def _gla_kernel(q_ref, k_ref, v_ref, r_ref, al_ref, wa2_ref, ba_ref, hnorm_ref, s0_ref, o_ref, sfin_ref,
                s_scr, b_scr, bnext_scr, o_scr, *, chunk, n_chunks, sub):
    step = pl.program_id(2)

    @pl.when(step == 0)
    def _load_state():
        s_scr[...] = s0_ref[0]

    c = chunk
    state_tile = 2 * _LANES
    n_sub = c // sub
    tri = (lax.broadcasted_iota(jnp.int32, (c, c), 0) >= lax.broadcasted_iota(jnp.int32, (c, c), 1)).astype(_BF16)
    sub_row = lax.broadcasted_iota(jnp.int32, (sub, c), 0)
    sub_col = lax.broadcasted_iota(jnp.int32, (sub, c), 1)

    q_scale = _DK ** -0.5
    trans_b = (((1,), (1,)), ((), ()))

    def gate_of_chunk(ci):
        rows = pl.ds(ci * c if isinstance(ci, int) else pl.multiple_of(ci * c, c), c)
        return jnp.dot(al_ref[rows, :].astype(_BF16), wa2_ref[...], preferred_element_type=_F32) + ba_ref[...]

    def decay_of_gate(x):
        g = (jnp.minimum(x, 0.0) - jnp.log(1.0 + jnp.exp(-jnp.abs(x)))) * (_LOG2E / _GATE_TAU)
        return _cumsum_rows(tri, g)

    def one_chunk(ci, carry):
        base = pl.multiple_of(ci * c, c)
        rows = pl.ds(base, c)
        b_scr[...] = bnext_scr[...]
        b_last = b_scr[c - 1:c, :]
        x_next = gate_of_chunk(jnp.minimum(ci + 1, n_chunks - 1))

        q_dec = (q_ref[rows, :].astype(_F32) * q_scale * jnp.exp2(b_scr[...])).astype(_BF16)
        o_scr[...] = jnp.dot(q_dec, s_scr[...].astype(_BF16), preferred_element_type=_F32)

        diag, off = [], [None]
        for blk in range(n_sub):
            r0 = blk * sub
            b_b = b_scr[r0:r0 + sub, :]
            q_b = q_ref[pl.ds(base + r0, sub), :].astype(_F32) * q_scale
            stacked = jnp.concatenate(
                [q_b * jnp.exp2(b_b - b_scr[r0 + j:r0 + j + 1, :]) for j in range(sub)], axis=0).astype(_BF16)
            diag.append(lax.dot_general(stacked, k_ref[rows, :], trans_b, preferred_element_type=_F32))
            if blk > 0:
                p = b_scr[r0 - 1:r0, :]
                q_o = (q_b * jnp.exp2(b_b - p)).astype(_BF16)
                k_o = (k_ref[rows, :].astype(_F32) * jnp.exp2(p - b_scr[...])).astype(_BF16)
                off.append(lax.dot_general(q_o, k_o, trans_b, preferred_element_type=_F32))

        b_next = decay_of_gate(x_next)

        tiles = []
        for blk in range(n_sub):
            r0 = blk * sub
            sc = jnp.zeros((sub, c), _F32)
            for j in range(sub):
                sc = jnp.where(sub_col == r0 + j, diag[blk][j * sub:(j + 1) * sub], sc)
            sc = jnp.where(sub_row + r0 >= sub_col, sc, 0.0)
            if blk > 0:
                sc = sc + jnp.where(sub_col < r0, off[blk], 0.0)
            tiles.append(sc)
        scores = jnp.concatenate(tiles, axis=0) if n_sub > 1 else tiles[0]
        o_intra = jnp.dot(scores.astype(_BF16), v_ref[rows, :], preferred_element_type=_F32)
        bnext_scr[...] = b_next

        k_dec = (k_ref[rows, :].astype(_F32) * jnp.exp2(b_last - b_scr[...])).astype(_BF16)
        decay_col = jnp.transpose(jnp.broadcast_to(jnp.exp2(b_last), (_LANES, _DK)))
        for t in range(_DV // state_tile):
            cols = slice(t * state_tile, (t + 1) * state_tile)
            upd = lax.dot_general(k_dec, v_ref[rows, cols], (((0,), (0,)), ((), ())), preferred_element_type=_F32)
            s_scr[:, cols] = s_scr[:, cols] * jnp.tile(decay_col, (1, state_tile // _LANES)) + upd

        o = o_scr[...] + o_intra
        ms = jnp.mean(o * o, axis=-1, keepdims=True)
        o_n = o * lax.rsqrt(ms + _EPS) * hnorm_ref[...]
        r = r_ref[rows, :].astype(_F32)
        o_ref[rows, :] = (o_n * (r * jax.nn.sigmoid(r))).astype(o_ref.dtype)
        return carry

    bnext_scr[...] = decay_of_gate(gate_of_chunk(0))
    lax.fori_loop(0, n_chunks, one_chunk, 0)

    @pl.when(step == pl.num_programs(2) - 1)
    def _store_state():
        sfin_ref[0, 0] = s_scr[...]


def _gla(proj, a_low, wa2, ba, hnorm, s0, *, n_batch, seq, chunk, rows_per_step):
    m = proj.shape[0]
    assert m == n_batch * seq and seq % rows_per_step == 0 and rows_per_step % chunk == 0
    steps = seq // rows_per_step
    r = rows_per_step
    kv = _QK // _DV

    def row(b, h, i):
        return b * steps + i

    kernel = functools.partial(_gla_kernel, chunk=chunk, n_chunks=r // chunk, sub=min(chunk, 8))
    blocks = [r * _DK * 2, r * _DK * 2, r * _DV * 2, r * _DV * 2, r * _LANES * 4, _LANES * _DK * 2,
              _DK * 4, _DV * 4, _DK * _DV * 4, r * _DV * 2, _DK * _DV * 4]
    outs = pl.pallas_call(
        kernel,
        grid=(n_batch, _HEADS, steps),
        in_specs=[
            pl.BlockSpec((r, _DK), lambda b, h, i: (row(b, h, i), h)),
            pl.BlockSpec((r, _DK), lambda b, h, i: (row(b, h, i), _HEADS + h)),
            pl.BlockSpec((r, _DV), lambda b, h, i: (row(b, h, i), 2 * kv + h)),
            pl.BlockSpec((r, _DV), lambda b, h, i: (row(b, h, i), 2 * kv + _HEADS + h)),
            pl.BlockSpec((r, _LANES), lambda b, h, i: (row(b, h, i), 0)),
            pl.BlockSpec((_LANES, _DK), lambda b, h, i: (0, h)),
            pl.BlockSpec((1, _DK), lambda b, h, i: (0, h)),
            pl.BlockSpec((1, _DV), lambda b, h, i: (0, 0)),
            pl.BlockSpec((1, _DK, _DV), lambda b, h, i: (h, 0, 0)),
        ],
        out_specs=[
            pl.BlockSpec((r, _DV), lambda b, h, i: (row(b, h, i), h)),
            pl.BlockSpec((1, 1, _DK, _DV), lambda b, h, i: (b, h, 0, 0)),
        ],
        out_shape=[
            jax.ShapeDtypeStruct((m, _V), _BF16),
            jax.ShapeDtypeStruct((n_batch, _HEADS, _DK, _DV), _F32),
        ],
        scratch_shapes=[pltpu.VMEM((_DK, _DV), _F32), pltpu.VMEM((chunk, _DK), _F32), pltpu.VMEM((chunk, _DK), _F32),
                        pltpu.VMEM((chunk, _DV), _F32)],
        compiler_params=_params(blocks, scratch_bytes=4 * _DK * _DV * 4, n_grid=3),
        name="gla",
    )(proj, proj, proj, proj, a_low, wa2, ba.reshape(1, _QK), hnorm.reshape(1, _DV), s0)
    return outs[0], outs[1]


def _conv_in_kernel(a_ref, wb_ref, wc_ref, wu_ref, s_ref, am_ref, sm_ref, cw_ref, o_ref, tail_scr, meta_scr,
                    *, blocks_per_seq):
    i, j = pl.program_id(0), pl.program_id(1)

    @pl.when(i == 0)
    def _meta_rows():
        am, sm = am_ref[...], sm_ref[...]
        z_meta = ((jnp.dot(am, wc_ref[0], preferred_element_type=_F32) * sm)
                  * (jnp.dot(am, wu_ref[0], preferred_element_type=_F32) * sm))
        meta_scr[j] = z_meta[_N_META - _SUBLANES:]
        tail_scr[j] = z_meta[_N_META - _SUBLANES:]

    a, s = a_ref[...], s_ref[...]
    bg = jnp.dot(a, wb_ref[0], preferred_element_type=_F32) * s
    z = ((jnp.dot(a, wc_ref[0], preferred_element_type=_F32) * s)
         * (jnp.dot(a, wu_ref[0], preferred_element_type=_F32) * s))
    prev = jnp.where(i % blocks_per_seq == 0, meta_scr[j], tail_scr[j])
    tail_scr[j] = z[z.shape[0] - _SUBLANES:]
    z_m1, z_m2 = prev[_SUBLANES - 1:_SUBLANES], prev[_SUBLANES - 2:_SUBLANES - 1]
    rid = lax.broadcasted_iota(jnp.int32, z.shape, 0)
    z1 = jnp.where(rid == 0, z_m1, pltpu.roll(z, 1, 0))
    z2 = jnp.where(rid == 0, z_m2, jnp.where(rid == 1, z_m1, pltpu.roll(z, 2, 0)))
    cw = cw_ref[...]
    conv = cw[2:3] * z + cw[0:1] * z2 + cw[1:2] * z1
    o_ref[...] = (bg * conv).astype(o_ref.dtype)


def _conv_in(a, rstd, a_meta, rstd_meta, w, conv_w, *, seq, bm, bn):
    m, k = a.shape
    assert m % bm == 0 and seq % bm == 0 and _D % bn == 0
    n_i, n_j = m // bm, _D // bn
    kernel = functools.partial(_conv_in_kernel, blocks_per_seq=seq // bm)
    blocks = [bm * k * 2, 3 * k * bn * 2, bm * _LANES * 4, _N_META * k * 2, _CONV_WIDTH * bn * 4, bm * bn * 2]
    return pl.pallas_call(
        kernel,
        grid=(n_i, n_j),
        in_specs=[
            pl.BlockSpec((bm, k), lambda i, j: (i, 0)),
            pl.BlockSpec((1, k, bn), lambda i, j: (j, 0, 0)),
            pl.BlockSpec((1, k, bn), lambda i, j: (n_j + j, 0, 0)),
            pl.BlockSpec((1, k, bn), lambda i, j: (2 * n_j + j, 0, 0)),
            pl.BlockSpec((bm, 1), lambda i, j: (i, 0)),
            pl.BlockSpec((_N_META, k), lambda i, j: (0, 0)),
            pl.BlockSpec((_N_META, 1), lambda i, j: (0, 0)),
            pl.BlockSpec((_CONV_WIDTH, bn), lambda i, j: (0, j)),
        ],
        out_specs=pl.BlockSpec((bm, bn), lambda i, j: (i, j)),
        out_shape=jax.ShapeDtypeStruct((m, _D), _BF16),
        scratch_shapes=[pltpu.VMEM((n_j, _SUBLANES, bn), _F32), pltpu.VMEM((n_j, _SUBLANES, bn), _F32)],
        compiler_params=_params(blocks, scratch_bytes=8 * bm * bn * 4),
        name="conv_in",
    )(a, w, w, w, rstd, a_meta, rstd_meta, conv_w)


def kernel(x, meta, norm_mix, norm_ffn, gla_w_in, gla_w_a2, gla_b_a, gla_head_norm, gla_w_out,
           conv_w_in, conv_w, conv_w_out, ffn_w_gate, ffn_w_up, ffn_w_down, norm_final):
    n_batch, seq, d = x.shape
    xm = x.reshape(n_batch * seq, d)
    xmeta = meta.astype(x.dtype)
    n_qkvr = 2 * _QK + 2 * _V
    d_ff = ffn_w_gate.shape[2]
    assert d_ff % 256 == 0


    w_qkvr = _cast_cols(gla_w_in, 0, norm_mix[0], 0, n_qkvr, rb=512, cb=1024)
    w_a1 = _cast_cols(gla_w_in, 0, norm_mix[0], n_qkvr, _RANK, rb=1024, cb=_LANES)
    w_a2 = jnp.pad(gla_w_a2[0], ((0, _LANES - _RANK), (0, 0))).astype(_BF16)

    hb, rstd = _row_stats(xm)
    hb_meta, rstd_meta = _row_stats(xmeta)
    proj, proj_meta, (w_out0, w_gate0) = _matmul(
        hb, [(w_qkvr, 0)], n_qkvr, _epilogue_cast, _BF16, bm=1024, bn=1024, scale=(rstd, rstd_meta),
        meta=(hb_meta, None),
        jobs=[_CastJob(gla_w_out, 0, 64, 512), _CastJob(ffn_w_gate, 0, 128, 256, norm_ffn[0])], name="gla_in")
    a_low, a_low_meta, _ = _matmul(hb, [(w_a1, 0)], _LANES, _epilogue_cast, _F32, bm=1024, bn=_LANES,
                                   scale=(rstd, rstd_meta), meta=(hb_meta, None), name="gla_gate_in")

    zero_state = jnp.zeros((_HEADS, _DK, _DV), _F32)
    o_meta, s_meta = _gla(proj_meta, a_low_meta, w_a2, gla_b_a[0], gla_head_norm[0], zero_state,
                          n_batch=1, seq=_N_META, chunk=_N_META, rows_per_step=_N_META)
    o_x, _ = _gla(proj, a_low, w_a2, gla_b_a[0], gla_head_norm[0], s_meta[0],
                  n_batch=n_batch, seq=seq, chunk=64, rows_per_step=512)
    h, hb, rstd, h_meta, (w_up0, w_down0, w_cin) = _matmul(
        o_x, [(w_out0, 0)], d, _epilogue_resid, _F32, bm=1024, bn=512, resid=xm, meta=(o_meta, xmeta),
        emit_stats=True,
        jobs=[_CastJob(ffn_w_up, 0, 128, 256, norm_ffn[0]), _CastJob(ffn_w_down, 0, d_ff // 128, 512),
              _CastJob(conv_w_in, 0, 128, 256, norm_mix[1])],
        name="gla_out")

    hb_meta, rstd_meta = _row_stats(h_meta)
    act, act_meta, (w_cout, w_gate1, w_up1) = _matmul(
        hb, [(w_gate0, 0), (w_up0, 0)], d_ff, _epilogue_swiglu, _BF16, bm=2048, bn=256, scale=(rstd, rstd_meta),
        meta=(hb_meta, None),
        jobs=[_CastJob(conv_w_out, 0, 64, 512), _CastJob(ffn_w_gate, 1, 256, 256, norm_ffn[1]),
              _CastJob(ffn_w_up, 1, 256, 256, norm_ffn[1])],
        name="ffn_gate_up")
    h, hb, rstd, h_meta, (w_down1,) = _matmul(
        act, [(w_down0, 0)], d, _epilogue_resid, _F32, bm=512, bn=512, resid=h, meta=(act_meta, h_meta),
        emit_stats=True, jobs=[_CastJob(ffn_w_down, 1, d_ff // 64, 512)], name="ffn_down")

    hb_meta, rstd_meta = _row_stats(h_meta)
    y = _conv_in(hb, rstd, hb_meta, rstd_meta, w_cin, conv_w[0], seq=seq, bm=1024, bn=256)
    h, hb, rstd, _, _ = _matmul(y, [(w_cout, 0)], d, _epilogue_resid, _F32, bm=1024, bn=512, resid=h,
                                emit_stats=True, name="conv_out")

    act, _, _ = _matmul(hb, [(w_gate1, 0), (w_up1, 0)], d_ff, _epilogue_swiglu, _BF16, bm=2048, bn=256,
                        scale=(rstd, None), name="ffn_gate_up")
    h, _, _ = _matmul(act, [(w_down1, 0)], d, _epilogue_resid, _F32, bm=512, bn=512, resid=h, name="ffn_down")
    out = _rmsnorm(h, norm_final, _F32)
    return out.reshape(n_batch, seq, d)
```

```python
import functools
from typing import NamedTuple

import jax
import jax.numpy as jnp
from jax import lax
from jax.experimental import pallas as pl
from jax.experimental.pallas import tpu as pltpu

_D = 4096
_N_META = 16
_HEADS = 4
_DK = 512
_DV = 1024
_QK = _HEADS * _DK
_V = _HEADS * _DV
_RANK = 16
_GATE_TAU = 16.0
_CONV_WIDTH = 3
_EPS = 1e-6
_LOG2E = 1.4426950408889634

_LANES = 128
_SUBLANES = 8
_BF16_SUBLANES = 16
_V7X_SCOPED_VMEM_BYTES = 60000 * 1024

_F32 = jnp.float32
_BF16 = jnp.bfloat16


def _params(block_bytes, scratch_bytes=0, n_grid=2):
    need = 2 * sum(block_bytes) + scratch_bytes
    assert need <= _V7X_SCOPED_VMEM_BYTES, (need, _V7X_SCOPED_VMEM_BYTES)
    return pltpu.CompilerParams(
        dimension_semantics=("arbitrary",) * n_grid,
        vmem_limit_bytes=_V7X_SCOPED_VMEM_BYTES,
    )


class _CastJob(NamedTuple):
    src: jax.Array
    layer: int
    n_blocks: int
    tile_cols: int
    gain: jax.Array = None

    @property
    def block(self):
        rows, cols = self.src.shape[1:]
        assert rows % self.n_blocks == 0 and (rows // self.n_blocks) % _BF16_SUBLANES == 0
        assert cols % self.tile_cols == 0
        return rows // self.n_blocks, cols


def _job_plumbing(jobs, n_steps, flat_step):
    in_specs, args, out_specs, out_shapes, blocks, desc = [], [], [], [], [], []
    for job in jobs:
        assert job.n_blocks <= n_steps, (job.n_blocks, n_steps)
        rb, cols = job.block
        last, layer = job.n_blocks - 1, job.layer

        def blk(*g, last=last):
            return jnp.minimum(flat_step(*g), last)

        in_specs.append(pl.BlockSpec((1, rb, cols), lambda *g, blk=blk, layer=layer: (layer, blk(*g), 0)))
        args.append(job.src)
        if job.gain is not None:
            in_specs.append(pl.BlockSpec((rb, 1), lambda *g, blk=blk: (blk(*g), 0)))
            args.append(job.gain.reshape(-1, 1))
        n_tiles, tc = cols // job.tile_cols, job.tile_cols
        out_specs.append(pl.BlockSpec((n_tiles, rb, tc), lambda *g, blk=blk: (0, blk(*g), 0)))
        out_shapes.append(jax.ShapeDtypeStruct((n_tiles, job.src.shape[1], tc), _BF16))
        blocks += [rb * cols * 4, rb * cols * 2, rb * _LANES * 4]
        desc.append(job.gain is not None)
    return in_specs, args, out_specs, out_shapes, blocks, tuple(desc)


def _take_job_inputs(it, desc):
    return [(next(it), next(it) if has_gain else None) for has_gain in desc]


def _run_casts(in_refs, dst_refs):
    for (src, gain), dst in zip(in_refs, dst_refs):
        w = src[0]
        if gain is not None:
            w = w * gain[...]
        w = w.astype(_BF16)
        n_tiles, _, tc = dst.shape
        for t in range(n_tiles):
            dst[t] = w[:, t * tc:(t + 1) * tc]


def _cast_cols_kernel(src_ref, gain_ref, dst_ref, *, n_valid):
    w = src_ref[0] * gain_ref[...]
    if n_valid < w.shape[1]:
        w = jnp.where(lax.broadcasted_iota(jnp.int32, w.shape, 1) < n_valid, w, 0.0)
    dst_ref[0] = w.astype(_BF16)


def _cast_cols(src, layer, gain, col0, n_cols, rb, cb):
    rows, src_cols = src.shape[1:]
    n_valid = min(n_cols, src_cols - col0)
    n_cols = -(-n_cols // cb) * cb
    assert rows % rb == 0 and col0 % cb == 0 and (n_valid == n_cols or n_cols == cb)
    c0 = col0 // cb
    return pl.pallas_call(
        functools.partial(_cast_cols_kernel, n_valid=n_valid),
        grid=(rows // rb, n_cols // cb),
        in_specs=[pl.BlockSpec((1, rb, cb), lambda i, j: (layer, i, c0 + j)),
                  pl.BlockSpec((rb, 1), lambda i, j: (i, 0))],
        out_specs=pl.BlockSpec((1, rb, cb), lambda i, j: (j, i, 0)),
        out_shape=jax.ShapeDtypeStruct((n_cols // cb, rows, cb), _BF16),
        compiler_params=_params([rb * cb * 4, rb * cb * 2, rb * _LANES * 4]),
        name="weight_cast",
    )(src, gain.reshape(-1, 1))


def _row_stats_kernel(x_ref, xb_ref, rstd_ref):
    x = x_ref[...]
    xb_ref[...] = x.astype(_BF16)
    rstd_ref[...] = lax.rsqrt(jnp.mean(x * x, axis=-1, keepdims=True) + _EPS)


def _row_stats(x, bm=256):
    m, d = x.shape
    bm = min(bm, m)
    assert m % bm == 0
    return pl.pallas_call(
        _row_stats_kernel,
        grid=(m // bm,),
        in_specs=[pl.BlockSpec((bm, d), lambda i: (i, 0))],
        out_specs=[pl.BlockSpec((bm, d), lambda i: (i, 0)), pl.BlockSpec((bm, 1), lambda i: (i, 0))],
        out_shape=[jax.ShapeDtypeStruct((m, d), _BF16), jax.ShapeDtypeStruct((m, 1), _F32)],
        compiler_params=_params([bm * d * 4, bm * d * 2, bm * _LANES * 4], scratch_bytes=2 * bm * d * 4, n_grid=1),
        name="row_stats",
    )(x)


def _rmsnorm_kernel(x_ref, g_ref, o_ref):
    x = x_ref[...]
    ms = jnp.mean(x * x, axis=-1, keepdims=True)
    o_ref[...] = (x * lax.rsqrt(ms + _EPS) * g_ref[...]).astype(o_ref.dtype)


def _rmsnorm(x, g, out_dtype, bm=256):
    m, d = x.shape
    bm = min(bm, m)
    assert m % bm == 0
    blocks = [bm * d * 4, d * 4, bm * d * jnp.dtype(out_dtype).itemsize]
    return pl.pallas_call(
        _rmsnorm_kernel,
        grid=(m // bm,),
        in_specs=[pl.BlockSpec((bm, d), lambda i: (i, 0)), pl.BlockSpec((1, d), lambda i: (0, 0))],
        out_specs=pl.BlockSpec((bm, d), lambda i: (i, 0)),
        out_shape=jax.ShapeDtypeStruct((m, d), out_dtype),
        compiler_params=_params(blocks, scratch_bytes=2 * bm * d * 4, n_grid=1),
        name="rmsnorm",
    )(x, g.reshape(1, d))


def _epilogue_cast(accs, resid):
    return accs[0]


def _epilogue_resid(accs, resid):
    return resid + accs[0]


def _epilogue_swiglu(accs, resid):
    gate, up = accs
    return gate * jax.nn.sigmoid(gate) * up


def _matmul_kernel(*refs, n_w, epilogue, has_resid, has_scale, has_meta, emit_stats, jobs, n_inner, n_cols,
                   row_split):
    it = iter(refs)
    a_ref = next(it)
    w_refs = [next(it) for _ in range(n_w)]
    r_ref = next(it) if has_resid else None
    s_ref = next(it) if has_scale else None
    am_ref = next(it) if has_meta else None
    rm_ref = next(it) if has_meta and has_resid else None
    sm_ref = next(it) if has_meta and has_scale else None
    job_in = _take_job_inputs(it, jobs)
    o_ref = next(it)
    ob_ref, rstd_ref = (next(it), next(it)) if emit_stats else (None, None)
    om_ref = next(it) if has_meta else None
    dst_refs = [next(it) for _ in jobs]
    ssq_scr = next(it) if emit_stats else None
    i, j = pl.program_id(0), pl.program_id(1)

    def result(a, resid, scale):
        accs = [jnp.dot(a, w[0], preferred_element_type=_F32) for w in w_refs]
        if scale is not None:
            accs = [acc * scale for acc in accs]
        return epilogue(accs, resid)

    sub_rows = o_ref.shape[0] // row_split
    parts = []
    for h in range(row_split):
        rows = slice(h * sub_rows, (h + 1) * sub_rows)
        out = result(a_ref[rows, :], r_ref[rows, :] if has_resid else None, s_ref[rows, :] if has_scale else None)
        o_ref[rows, :] = out.astype(o_ref.dtype)
        if emit_stats:
            ob_ref[rows, :] = out.astype(_BF16)
            parts.append(jnp.sum(out * out, axis=-1, keepdims=True))
        _run_casts(job_in[h::row_split], dst_refs[h::row_split])

    if emit_stats:
        part = jnp.concatenate(parts, axis=0) if row_split > 1 else parts[0]

        @pl.when(j == 0)
        def _first():
            ssq_scr[...] = part

        @pl.when(j != 0)
        def _accumulate():
            ssq_scr[...] += part

        @pl.when(j == n_inner - 1)
        def _finish():
            rstd_ref[...] = lax.rsqrt(ssq_scr[...] * (1.0 / n_cols) + _EPS)

    if has_meta:
        @pl.when(i == 0)
        def _meta_rows():
            out_m = result(am_ref[...], rm_ref[...] if has_resid else None, sm_ref[...] if has_scale else None)
            om_ref[0] = out_m.astype(om_ref.dtype)

        @pl.when(i != 0)
        def _meta_unused():
            om_ref[0] = jnp.zeros(om_ref.shape[1:], om_ref.dtype)


def _matmul(a, ws, n_out, epilogue, out_dtype, bm, bn, resid=None, scale=None, meta=None, emit_stats=False,
            jobs=(), row_split=1, name="matmul"):
    m, k = a.shape
    bm = min(bm, m)
    assert m % bm == 0 and n_out % bn == 0
    n_i, n_j = m // bm, n_out // bn
    out_bytes = jnp.dtype(out_dtype).itemsize
    has_resid, has_scale, has_meta = resid is not None, scale is not None, meta is not None

    in_specs = [pl.BlockSpec((bm, k), lambda i, j: (i, 0))]
    args = [a]
    blocks = [bm * k * 2, bm * bn * out_bytes]
    for w, off in ws:
        assert w.shape[1:] == (k, bn), (w.shape, k, bn)
        in_specs.append(pl.BlockSpec((1, k, bn), lambda i, j, off=off: (off + j, 0, 0)))
        args.append(w)
        blocks.append(k * bn * 2)
    if has_resid:
        in_specs.append(pl.BlockSpec((bm, bn), lambda i, j: (i, j)))
        args.append(resid)
        blocks.append(bm * bn * 4)
    if has_scale:
        in_specs.append(pl.BlockSpec((bm, 1), lambda i, j: (i, 0)))
        args.append(scale[0])
        blocks.append(bm * _LANES * 4)
    out_specs = [pl.BlockSpec((bm, bn), lambda i, j: (i, j))]
    out_shapes = [jax.ShapeDtypeStruct((m, n_out), out_dtype)]
    scratch = []
    if emit_stats:
        out_specs += [pl.BlockSpec((bm, bn), lambda i, j: (i, j)), pl.BlockSpec((bm, 1), lambda i, j: (i, 0))]
        out_shapes += [jax.ShapeDtypeStruct((m, n_out), _BF16), jax.ShapeDtypeStruct((m, 1), _F32)]
        blocks += [bm * bn * 2, bm * _LANES * 4]
        scratch.append(pltpu.VMEM((bm, 1), _F32))
    if has_meta:
        a_meta, r_meta = meta
        in_specs.append(pl.BlockSpec((_N_META, k), lambda i, j: (0, 0)))
        args.append(a_meta)
        blocks.append(_N_META * k * 2)
        if has_resid:
            in_specs.append(pl.BlockSpec((_N_META, bn), lambda i, j: (0, j)))
            args.append(r_meta)
        if has_scale:
            in_specs.append(pl.BlockSpec((_N_META, 1), lambda i, j: (0, 0)))
            args.append(scale[1])
        out_specs.append(pl.BlockSpec((1, _N_META, bn), lambda i, j: (i, 0, j)))
        out_shapes.append(jax.ShapeDtypeStruct((n_i, _N_META, n_out), out_dtype))
    j_in, j_args, j_out, j_shapes, j_blocks, j_desc = _job_plumbing(jobs, n_i * n_j, lambda i, j: i * n_j + j)

    kernel = functools.partial(_matmul_kernel, n_w=len(ws), epilogue=epilogue, has_resid=has_resid,
                               has_scale=has_scale, has_meta=has_meta, emit_stats=emit_stats, jobs=j_desc,
                               n_inner=n_j, n_cols=n_out, row_split=row_split)
    outs = pl.pallas_call(
        kernel,
        grid=(n_i, n_j),
        in_specs=in_specs + j_in,
        out_specs=out_specs + j_out,
        out_shape=out_shapes + j_shapes,
        scratch_shapes=scratch,
        compiler_params=_params(blocks + j_blocks, scratch_bytes=(len(ws) + 1 + emit_stats) * bm * bn * 4),
        name=name,
    )(*args, *j_args)
    n_main = 3 if emit_stats else 1
    main = list(outs[:n_main])
    out_meta = outs[n_main][0] if has_meta else None
    return (*main, out_meta, list(outs[n_main + has_meta:]))


def _cumsum_rows(tri, g):
    g1 = g.astype(_BF16)
    r1 = g - g1.astype(_F32)
    g2 = r1.astype(_BF16)
    g3 = (r1 - g2.astype(_F32)).astype(_BF16)
    out = jnp.dot(tri, g1, preferred_element_type=_F32)
    out = out + jnp.dot(tri, g2, preferred_element_type=_F32)
    return out + jnp.dot(tri, g3, preferred_element_type=_F32)


def _gla_kernel(q_ref, k_ref, v_ref, r_ref, al_ref, wa2_ref, ba_ref, hnorm_ref, s0_ref, o_ref, sfin_ref,
                s_scr, b_scr, bnext_scr, o_scr, *, chunk, n_chunks, sub):
    step = pl.program_id(2)

    @pl.when(step == 0)
    def _load_state():
        s_scr[...] = s0_ref[0]

    c = chunk
    state_tile = 2 * _LANES
    n_sub = c // sub
    tri = (lax.broadcasted_iota(jnp.int32, (c, c), 0) >= lax.broadcasted_iota(jnp.int32, (c, c), 1)).astype(_BF16)
    sub_row = lax.broadcasted_iota(jnp.int32, (sub, c), 0)
    sub_col = lax.broadcasted_iota(jnp.int32, (sub, c), 1)

    q_scale = _DK ** -0.5
    trans_b = (((1,), (1,)), ((), ()))

    def gate_of_chunk(ci):
        rows = pl.ds(ci * c if isinstance(ci, int) else pl.multiple_of(ci * c, c), c)
        return jnp.dot(al_ref[rows, :].astype(_BF16), wa2_ref[...], preferred_element_type=_F32) + ba_ref[...]

    def decay_of_gate(x):
        g = (jnp.minimum(x, 0.0) - jnp.log(1.0 + jnp.exp(-jnp.abs(x)))) * (_LOG2E / _GATE_TAU)
        return _cumsum_rows(tri, g)

    def one_chunk(ci, carry):
        base = pl.multiple_of(ci * c, c)
        rows = pl.ds(base, c)
        b_scr[...] = bnext_scr[...]
        b_last = b_scr[c - 1:c, :]
        x_next = gate_of_chunk(jnp.minimum(ci + 1, n_chunks - 1))

        q_dec = (q_ref[rows, :].astype(_F32) * q_scale * jnp.exp2(b_scr[...])).astype(_BF16)
        o_scr[...] = jnp.dot(q_dec, s_scr[...].astype(_BF16), preferred_element_type=_F32)

        diag, off = [], [None]
        for blk in range(n_sub):
            r0 = blk * sub
            b_b = b_scr[r0:r0 + sub, :]
            q_b = q_ref[pl.ds(base + r0, sub), :].astype(_F32) * q_scale
            stacked = jnp.concatenate(
                [q_b * jnp.exp2(b_b - b_scr[r0 + j:r0 + j + 1, :]) for j in range(sub)], axis=0).astype(_BF16)
            diag.append(lax.dot_general(stacked, k_ref[rows, :], trans_b, preferred_element_type=_F32))
            if blk > 0:
                p = b_scr[r0 - 1:r0, :]
                q_o = (q_b * jnp.exp2(b_b - p)).astype(_BF16)
                k_o = (k_ref[rows, :].astype(_F32) * jnp.exp2(p - b_scr[...])).astype(_BF16)
                off.append(lax.dot_general(q_o, k_o, trans_b, preferred_element_type=_F32))

        b_next = decay_of_gate(x_next)

        tiles = []
        for blk in range(n_sub):
            r0 = blk * sub
            sc = jnp.zeros((sub, c), _F32)
            for j in range(sub):
                sc = jnp.where(sub_col == r0 + j, diag[blk][j * sub:(j + 1) * sub], sc)
            sc = jnp.where(sub_row + r0 >= sub_col, sc, 0.0)
            if blk > 0:
                sc = sc + jnp.where(sub_col < r0, off[blk], 0.0)
            tiles.append(sc)
        scores = jnp.concatenate(tiles, axis=0) if n_sub > 1 else tiles[0]
        o_intra = jnp.dot(scores.astype(_BF16), v_ref[rows, :], preferred_element_type=_F32)
        bnext_scr[...] = b_next

        k_dec = (k_ref[rows, :].astype(_F32) * jnp.exp2(b_last - b_scr[...])).astype(_BF16)
        decay_col = jnp.transpose(jnp.broadcast_to(jnp.exp2(b_last), (_LANES, _DK)))
        for t in range(_DV // state_tile):
            cols = slice(t * state_tile, (t + 1) * state_tile)
            upd = lax.dot_general(k_dec, v_ref[rows, cols], (((0,), (0,)), ((), ())), preferred_element_type=_F32)
            s_scr[:, cols] = s_scr[:, cols] * jnp.tile(decay_col, (1, state_tile // _LANES)) + upd

        o = o_scr[...] + o_intra
        ms = jnp.mean(o * o, axis=-1, keepdims=True)
        o_n = o * lax.rsqrt(ms + _EPS) * hnorm_ref[...]
        r = r_ref[rows, :].astype(_F32)
        o_ref[rows, :] = (o_n * (r * jax.nn.sigmoid(r))).astype(o_ref.dtype)
        return carry

    bnext_scr[...] = decay_of_gate(gate_of_chunk(0))
    lax.fori_loop(0, n_chunks, one_chunk, 0)

    @pl.when(step == pl.num_programs(2) - 1)
    def _store_state():
        sfin_ref[0, 0] = s_scr[...]


def _gla(proj, a_low, wa2, ba, hnorm, s0, *, n_batch, seq, chunk, rows_per_step):
    m = proj.shape[0]
    assert m == n_batch * seq and seq % rows_per_step == 0 and rows_per_step % chunk == 0
    steps = seq // rows_per_step
    r = rows_per_step
    kv = _QK // _DV

    def row(b, h, i):
        return b * steps + i

    kernel = functools.partial(_gla_kernel, chunk=chunk, n_chunks=r // chunk, sub=min(chunk, 8))
    blocks = [r * _DK * 2, r * _DK * 2, r * _DV * 2, r * _DV * 2, r * _LANES * 4, _LANES * _DK * 2,
              _DK * 4, _DV * 4, _DK * _DV * 4, r * _DV * 2, _DK * _DV * 4]
    outs = pl.pallas_call(
        kernel,
        grid=(n_batch, _HEADS, steps),
        in_specs=[
            pl.BlockSpec((r, _DK), lambda b, h, i: (row(b, h, i), h)),
            pl.BlockSpec((r, _DK), lambda b, h, i: (row(b, h, i), _HEADS + h)),
            pl.BlockSpec((r, _DV), lambda b, h, i: (row(b, h, i), 2 * kv + h)),
            pl.BlockSpec((r, _DV), lambda b, h, i: (row(b, h, i), 2 * kv + _HEADS + h)),
            pl.BlockSpec((r, _LANES), lambda b, h, i: (row(b, h, i), 0)),
            pl.BlockSpec((_LANES, _DK), lambda b, h, i: (0, h)),
            pl.BlockSpec((1, _DK), lambda b, h, i: (0, h)),
            pl.BlockSpec((1, _DV), lambda b, h, i: (0, 0)),
            pl.BlockSpec((1, _DK, _DV), lambda b, h, i: (h, 0, 0)),
        ],
        out_specs=[
            pl.BlockSpec((r, _DV), lambda b, h, i: (row(b, h, i), h)),
            pl.BlockSpec((1, 1, _DK, _DV), lambda b, h, i: (b, h, 0, 0)),
        ],
        out_shape=[
            jax.ShapeDtypeStruct((m, _V), _BF16),
            jax.ShapeDtypeStruct((n_batch, _HEADS, _DK, _DV), _F32),
        ],
        scratch_shapes=[pltpu.VMEM((_DK, _DV), _F32), pltpu.VMEM((chunk, _DK), _F32), pltpu.VMEM((chunk, _DK), _F32),
                        pltpu.VMEM((chunk, _DV), _F32)],
        compiler_params=_params(blocks, scratch_bytes=4 * _DK * _DV * 4, n_grid=3),
        name="gla",
    )(proj, proj, proj, proj, a_low, wa2, ba.reshape(1, _QK), hnorm.reshape(1, _DV), s0)
    return outs[0], outs[1]


def _conv_in_kernel(a_ref, wb_ref, wc_ref, wu_ref, s_ref, am_ref, sm_ref, cw_ref, o_ref, tail_scr, meta_scr,
                    *, blocks_per_seq, row_split):
    i, j = pl.program_id(0), pl.program_id(1)

    @pl.when(i == 0)
    def _meta_rows():
        am, sm = am_ref[...], sm_ref[...]
        z_meta = ((jnp.dot(am, wc_ref[0], preferred_element_type=_F32) * sm)
                  * (jnp.dot(am, wu_ref[0], preferred_element_type=_F32) * sm))
        meta_scr[j] = z_meta[_N_META - _SUBLANES:]
        tail_scr[j] = z_meta[_N_META - _SUBLANES:]

    prev = jnp.where(i % blocks_per_seq == 0, meta_scr[j], tail_scr[j])
    cw = cw_ref[...]
    sub_rows = o_ref.shape[0] // row_split
    rid = lax.broadcasted_iota(jnp.int32, (sub_rows, o_ref.shape[1]), 0)
    for h in range(row_split):
        rows = slice(h * sub_rows, (h + 1) * sub_rows)
        a, s = a_ref[rows, :], s_ref[rows, :]
        bg = jnp.dot(a, wb_ref[0], preferred_element_type=_F32) * s
        z = ((jnp.dot(a, wc_ref[0], preferred_element_type=_F32) * s)
             * (jnp.dot(a, wu_ref[0], preferred_element_type=_F32) * s))
        z_m1, z_m2 = prev[_SUBLANES - 1:_SUBLANES], prev[_SUBLANES - 2:_SUBLANES - 1]
        z1 = jnp.where(rid == 0, z_m1, pltpu.roll(z, 1, 0))
        z2 = jnp.where(rid == 0, z_m2, jnp.where(rid == 1, z_m1, pltpu.roll(z, 2, 0)))
        conv = cw[2:3] * z + cw[0:1] * z2 + cw[1:2] * z1
        o_ref[rows, :] = (bg * conv).astype(o_ref.dtype)
        prev = z[sub_rows - _SUBLANES:]
    tail_scr[j] = prev


def _conv_in(a, rstd, a_meta, rstd_meta, w, conv_w, *, seq, bm, bn, row_split):
    m, k = a.shape
    assert m % bm == 0 and seq % bm == 0 and _D % bn == 0
    n_i, n_j = m // bm, _D // bn
    kernel = functools.partial(_conv_in_kernel, blocks_per_seq=seq // bm, row_split=row_split)
    blocks = [bm * k * 2, 3 * k * bn * 2, bm * _LANES * 4, _N_META * k * 2, _CONV_WIDTH * bn * 4, bm * bn * 2]
    return pl.pallas_call(
        kernel,
        grid=(n_i, n_j),
        in_specs=[
            pl.BlockSpec((bm, k), lambda i, j: (i, 0)),
            pl.BlockSpec((1, k, bn), lambda i, j: (j, 0, 0)),
            pl.BlockSpec((1, k, bn), lambda i, j: (n_j + j, 0, 0)),
            pl.BlockSpec((1, k, bn), lambda i, j: (2 * n_j + j, 0, 0)),
            pl.BlockSpec((bm, 1), lambda i, j: (i, 0)),
            pl.BlockSpec((_N_META, k), lambda i, j: (0, 0)),
            pl.BlockSpec((_N_META, 1), lambda i, j: (0, 0)),
            pl.BlockSpec((_CONV_WIDTH, bn), lambda i, j: (0, j)),
        ],
        out_specs=pl.BlockSpec((bm, bn), lambda i, j: (i, j)),
        out_shape=jax.ShapeDtypeStruct((m, _D), _BF16),
        scratch_shapes=[pltpu.VMEM((n_j, _SUBLANES, bn), _F32), pltpu.VMEM((n_j, _SUBLANES, bn), _F32)],
        compiler_params=_params(blocks, scratch_bytes=8 * bm * bn * 4),
        name="conv_in",
    )(a, w, w, w, rstd, a_meta, rstd_meta, conv_w)


def kernel(x, meta, norm_mix, norm_ffn, gla_w_in, gla_w_a2, gla_b_a, gla_head_norm, gla_w_out,
           conv_w_in, conv_w, conv_w_out, ffn_w_gate, ffn_w_up, ffn_w_down, norm_final):
    n_batch, seq, d = x.shape
    xm = x.reshape(n_batch * seq, d)
    xmeta = meta.astype(x.dtype)
    n_qkvr = 2 * _QK + 2 * _V
    d_ff = ffn_w_gate.shape[2]
    assert d_ff % 256 == 0


    w_qkvr = _cast_cols(gla_w_in, 0, norm_mix[0], 0, n_qkvr, rb=512, cb=1024)
    w_a1 = _cast_cols(gla_w_in, 0, norm_mix[0], n_qkvr, _RANK, rb=1024, cb=_LANES)
    w_a2 = jnp.pad(gla_w_a2[0], ((0, _LANES - _RANK), (0, 0))).astype(_BF16)

    hb, rstd = _row_stats(xm)
    hb_meta, rstd_meta = _row_stats(xmeta)
    proj, proj_meta, (w_out0, w_gate0, w_up0) = _matmul(
        hb, [(w_qkvr, 0)], n_qkvr, _epilogue_cast, _BF16, bm=1024, bn=1024, scale=(rstd, rstd_meta),
        meta=(hb_meta, None),
        jobs=[_CastJob(gla_w_out, 0, 64, 512), _CastJob(ffn_w_gate, 0, 128, 256, norm_ffn[0]),
              _CastJob(ffn_w_up, 0, 128, 256, norm_ffn[0])],
        row_split=4, name="gla_in")
    a_low, a_low_meta, _ = _matmul(hb, [(w_a1, 0)], _LANES, _epilogue_cast, _F32, bm=1024, bn=_LANES,
                                   scale=(rstd, rstd_meta), meta=(hb_meta, None), name="gla_gate_in")

    zero_state = jnp.zeros((_HEADS, _DK, _DV), _F32)
    o_meta, s_meta = _gla(proj_meta, a_low_meta, w_a2, gla_b_a[0], gla_head_norm[0], zero_state,
                          n_batch=1, seq=_N_META, chunk=_N_META, rows_per_step=_N_META)
    o_x, _ = _gla(proj, a_low, w_a2, gla_b_a[0], gla_head_norm[0], s_meta[0],
                  n_batch=n_batch, seq=seq, chunk=64, rows_per_step=512)
    h, hb, rstd, h_meta, (w_down0,) = _matmul(
        o_x, [(w_out0, 0)], d, _epilogue_resid, _F32, bm=1024, bn=512, resid=xm, meta=(o_meta, xmeta),
        emit_stats=True, jobs=[_CastJob(ffn_w_down, 0, d_ff // 128, 512)], row_split=4, name="gla_out")

    hb_meta, rstd_meta = _row_stats(h_meta)
    act, act_meta, (w_cout, w_gate1, w_up1) = _matmul(
        hb, [(w_gate0, 0), (w_up0, 0)], d_ff, _epilogue_swiglu, _BF16, bm=2048, bn=256, scale=(rstd, rstd_meta),
        meta=(hb_meta, None),
        jobs=[_CastJob(conv_w_out, 0, 64, 512), _CastJob(ffn_w_gate, 1, 256, 256, norm_ffn[1]),
              _CastJob(ffn_w_up, 1, 256, 256, norm_ffn[1])],
        row_split=8, name="ffn_gate_up")
    h, hb, rstd, h_meta, (w_down1, w_cin) = _matmul(
        act, [(w_down0, 0)], d, _epilogue_resid, _F32, bm=512, bn=512, resid=h, meta=(act_meta, h_meta),
        emit_stats=True,
        jobs=[_CastJob(ffn_w_down, 1, d_ff // 64, 512), _CastJob(conv_w_in, 0, 256, 256, norm_mix[1])],
        row_split=2, name="ffn_down")

    hb_meta, rstd_meta = _row_stats(h_meta)
    y = _conv_in(hb, rstd, hb_meta, rstd_meta, w_cin, conv_w[0], seq=seq, bm=1024, bn=256, row_split=8)
    h, hb, rstd, _, _ = _matmul(y, [(w_cout, 0)], d, _epilogue_resid, _F32, bm=1024, bn=512, resid=h,
                                emit_stats=True, row_split=4, name="conv_out")

    act, _, _ = _matmul(hb, [(w_gate1, 0), (w_up1, 0)], d_ff, _epilogue_swiglu, _BF16, bm=2048, bn=256,
                        scale=(rstd, None), row_split=8, name="ffn_gate_up")
    h, _, _ = _matmul(act, [(w_down1, 0)], d, _epilogue_resid, _F32, bm=512, bn=512, resid=h, row_split=2,
                      name="ffn_down")
    out = _rmsnorm(h, norm_final, _F32)
    return out.reshape(n_batch, seq, d)
```

```python
import functools
from typing import NamedTuple

import jax
import jax.numpy as jnp
from jax import lax
from jax.experimental import pallas as pl
from jax.experimental.pallas import tpu as pltpu

_D = 4096
_N_META = 16
_HEADS = 4
_DK = 512
_DV = 1024
_QK = _HEADS * _DK
_V = _HEADS * _DV
_RANK = 16
_GATE_TAU = 16.0
_CONV_WIDTH = 3
_EPS = 1e-6
_LOG2E = 1.4426950408889634

_LANES = 128
_SUBLANES = 8
_BF16_SUBLANES = 16
_V7X_SCOPED_VMEM_BYTES = 60000 * 1024

_F32 = jnp.float32
_BF16 = jnp.bfloat16


def _params(block_bytes, scratch_bytes=0, n_grid=2):
    need = 2 * sum(block_bytes) + scratch_bytes
    assert need <= _V7X_SCOPED_VMEM_BYTES, (need, _V7X_SCOPED_VMEM_BYTES)
    return pltpu.CompilerParams(
        dimension_semantics=("arbitrary",) * n_grid,
        vmem_limit_bytes=_V7X_SCOPED_VMEM_BYTES,
    )


class _CastJob(NamedTuple):
    src: jax.Array
    layer: int
    n_blocks: int
    tile_cols: int
    gain: jax.Array = None

    @property
    def block(self):
        rows, cols = self.src.shape[1:]
        assert rows % self.n_blocks == 0 and (rows // self.n_blocks) % _BF16_SUBLANES == 0
        assert cols % self.tile_cols == 0
        return rows // self.n_blocks, cols


def _job_plumbing(jobs, n_steps, flat_step):
    in_specs, args, out_specs, out_shapes, blocks, desc = [], [], [], [], [], []
    for job in jobs:
        assert job.n_blocks <= n_steps, (job.n_blocks, n_steps)
        rb, cols = job.block
        last, layer = job.n_blocks - 1, job.layer

        def blk(*g, last=last):
            return jnp.minimum(flat_step(*g), last)

        in_specs.append(pl.BlockSpec((1, rb, cols), lambda *g, blk=blk, layer=layer: (layer, blk(*g), 0)))
        args.append(job.src)
        if job.gain is not None:
            in_specs.append(pl.BlockSpec((rb, 1), lambda *g, blk=blk: (blk(*g), 0)))
            args.append(job.gain.reshape(-1, 1))
        n_tiles, tc = cols // job.tile_cols, job.tile_cols
        out_specs.append(pl.BlockSpec((n_tiles, rb, tc), lambda *g, blk=blk: (0, blk(*g), 0)))
        out_shapes.append(jax.ShapeDtypeStruct((n_tiles, job.src.shape[1], tc), _BF16))
        blocks += [rb * cols * 4, rb * cols * 2, rb * _LANES * 4]
        desc.append(job.gain is not None)
    return in_specs, args, out_specs, out_shapes, blocks, tuple(desc)


def _take_job_inputs(it, desc):
    return [(next(it), next(it) if has_gain else None) for has_gain in desc]


def _run_casts(in_refs, dst_refs):
    for (src, gain), dst in zip(in_refs, dst_refs):
        w = src[0]
        if gain is not None:
            w = w * gain[...]
        w = w.astype(_BF16)
        n_tiles, _, tc = dst.shape
        for t in range(n_tiles):
            dst[t] = w[:, t * tc:(t + 1) * tc]


def _cast_rows_kernel(src_ref, gain_ref, dst_ref, *, n_valid):
    w = src_ref[0] * gain_ref[...]
    if n_valid < w.shape[0]:
        w = jnp.where(lax.broadcasted_iota(jnp.int32, w.shape, 0) < n_valid, w, 0.0)
    dst_ref[0] = jnp.transpose(w).astype(_BF16)


def _cast_rows(src_t, layer, gain, row0, n_rows, tile, rb):
    src_rows, cols = src_t.shape[1:]
    n_valid = min(n_rows, src_rows - row0)
    n_tiles = -(-n_rows // tile)
    per_tile = tile // rb
    assert tile % rb == 0 and row0 % rb == 0 and (n_valid == n_tiles * tile or (n_tiles == 1 and rb == tile))
    r0 = row0 // rb
    return pl.pallas_call(
        functools.partial(_cast_rows_kernel, n_valid=n_valid),
        grid=(n_tiles * per_tile,),
        in_specs=[pl.BlockSpec((1, rb, cols), lambda i: (layer, r0 + i, 0)),
                  pl.BlockSpec((1, cols), lambda i: (0, 0))],
        out_specs=pl.BlockSpec((1, cols, rb), lambda i: (i // per_tile, 0, i % per_tile)),
        out_shape=jax.ShapeDtypeStruct((n_tiles, cols, tile), _BF16),
        compiler_params=_params([rb * cols * 4, rb * cols * 2, cols * 4], scratch_bytes=2 * rb * cols * 4, n_grid=1),
        name="weight_cast",
    )(src_t, gain.reshape(1, -1))


def _row_stats_kernel(x_ref, xb_ref, rstd_ref):
    x = x_ref[...]
    xb_ref[...] = x.astype(_BF16)
    rstd_ref[...] = lax.rsqrt(jnp.mean(x * x, axis=-1, keepdims=True) + _EPS)


def _row_stats(x, bm=256):
    m, d = x.shape
    bm = min(bm, m)
    assert m % bm == 0
    return pl.pallas_call(
        _row_stats_kernel,
        grid=(m // bm,),
        in_specs=[pl.BlockSpec((bm, d), lambda i: (i, 0))],
        out_specs=[pl.BlockSpec((bm, d), lambda i: (i, 0)), pl.BlockSpec((bm, 1), lambda i: (i, 0))],
        out_shape=[jax.ShapeDtypeStruct((m, d), _BF16), jax.ShapeDtypeStruct((m, 1), _F32)],
        compiler_params=_params([bm * d * 4, bm * d * 2, bm * _LANES * 4], scratch_bytes=2 * bm * d * 4, n_grid=1),
        name="row_stats",
    )(x)


def _rmsnorm_kernel(x_ref, g_ref, o_ref):
    x = x_ref[...]
    ms = jnp.mean(x * x, axis=-1, keepdims=True)
    o_ref[...] = (x * lax.rsqrt(ms + _EPS) * g_ref[...]).astype(o_ref.dtype)


def _rmsnorm(x, g, out_dtype, bm=256):
    m, d = x.shape
    bm = min(bm, m)
    assert m % bm == 0
    blocks = [bm * d * 4, d * 4, bm * d * jnp.dtype(out_dtype).itemsize]
    return pl.pallas_call(
        _rmsnorm_kernel,
        grid=(m // bm,),
        in_specs=[pl.BlockSpec((bm, d), lambda i: (i, 0)), pl.BlockSpec((1, d), lambda i: (0, 0))],
        out_specs=pl.BlockSpec((bm, d), lambda i: (i, 0)),
        out_shape=jax.ShapeDtypeStruct((m, d), out_dtype),
        compiler_params=_params(blocks, scratch_bytes=2 * bm * d * 4, n_grid=1),
        name="rmsnorm",
    )(x, g.reshape(1, d))


def _epilogue_cast(accs, resid):
    return accs[0]


def _epilogue_resid(accs, resid):
    return resid + accs[0]


def _epilogue_swiglu(accs, resid):
    gate, up = accs
    return gate * jax.nn.sigmoid(gate) * up


def _matmul_kernel(*refs, w_transposed, epilogue, has_resid, has_scale, has_meta, emit_stats, jobs, n_inner,
                   n_cols, row_split):
    it = iter(refs)
    a_ref = next(it)
    w_refs = [next(it) for _ in w_transposed]
    r_ref = next(it) if has_resid else None
    s_ref = next(it) if has_scale else None
    am_ref = next(it) if has_meta else None
    rm_ref = next(it) if has_meta and has_resid else None
    sm_ref = next(it) if has_meta and has_scale else None
    job_in = _take_job_inputs(it, jobs)
    o_ref = next(it)
    ob_ref, rstd_ref = (next(it), next(it)) if emit_stats else (None, None)
    om_ref = next(it) if has_meta else None
    dst_refs = [next(it) for _ in jobs]
    ssq_scr = next(it) if emit_stats else None
    i, j = pl.program_id(0), pl.program_id(1)

    def result(a, resid, scale):
        accs = [lax.dot_general(a, w[0], (((1,), (1 if t else 0,)), ((), ())), preferred_element_type=_F32)
                for w, t in zip(w_refs, w_transposed)]
        if scale is not None:
            accs = [acc * scale for acc in accs]
        return epilogue(accs, resid)

    sub_rows = o_ref.shape[0] // row_split
    parts = []
    for h in range(row_split):
        rows = slice(h * sub_rows, (h + 1) * sub_rows)
        out = result(a_ref[rows, :], r_ref[rows, :] if has_resid else None, s_ref[rows, :] if has_scale else None)
        o_ref[rows, :] = out.astype(o_ref.dtype)
        if emit_stats:
            ob_ref[rows, :] = out.astype(_BF16)
            parts.append(jnp.sum(out * out, axis=-1, keepdims=True))
        _run_casts(job_in[h::row_split], dst_refs[h::row_split])

    if emit_stats:
        part = jnp.concatenate(parts, axis=0) if row_split > 1 else parts[0]

        @pl.when(j == 0)
        def _first():
            ssq_scr[...] = part

        @pl.when(j != 0)
        def _accumulate():
            ssq_scr[...] += part

        @pl.when(j == n_inner - 1)
        def _finish():
            rstd_ref[...] = lax.rsqrt(ssq_scr[...] * (1.0 / n_cols) + _EPS)

    if has_meta:
        @pl.when(i == 0)
        def _meta_rows():
            out_m = result(am_ref[...], rm_ref[...] if has_resid else None, sm_ref[...] if has_scale else None)
            om_ref[0] = out_m.astype(om_ref.dtype)

        @pl.when(i != 0)
        def _meta_unused():
            om_ref[0] = jnp.zeros(om_ref.shape[1:], om_ref.dtype)


def _matmul(a, ws, n_out, epilogue, out_dtype, bm, bn, resid=None, scale=None, meta=None, emit_stats=False,
            jobs=(), row_split=1, name="matmul"):
    m, k = a.shape
    bm = min(bm, m)
    assert m % bm == 0 and n_out % bn == 0
    n_i, n_j = m // bm, n_out // bn
    out_bytes = jnp.dtype(out_dtype).itemsize
    has_resid, has_scale, has_meta = resid is not None, scale is not None, meta is not None

    in_specs = [pl.BlockSpec((bm, k), lambda i, j: (i, 0))]
    args = [a]
    blocks = [bm * k * 2, bm * bn * out_bytes]
    w_transposed = tuple(len(entry) > 2 and entry[2] for entry in ws)
    for (w, off, *_), transposed in zip(ws, w_transposed):
        tile = (bn, k) if transposed else (k, bn)
        assert w.shape[1:] == tile, (w.shape, tile)
        in_specs.append(pl.BlockSpec((1, *tile), lambda i, j, off=off: (off + j, 0, 0)))
        args.append(w)
        blocks.append(k * bn * 2)
    if has_resid:
        in_specs.append(pl.BlockSpec((bm, bn), lambda i, j: (i, j)))
        args.append(resid)
        blocks.append(bm * bn * 4)
    if has_scale:
        in_specs.append(pl.BlockSpec((bm, 1), lambda i, j: (i, 0)))
        args.append(scale[0])
        blocks.append(bm * _LANES * 4)
    out_specs = [pl.BlockSpec((bm, bn), lambda i, j: (i, j))]
    out_shapes = [jax.ShapeDtypeStruct((m, n_out), out_dtype)]
    scratch = []
    if emit_stats:
        out_specs += [pl.BlockSpec((bm, bn), lambda i, j: (i, j)), pl.BlockSpec((bm, 1), lambda i, j: (i, 0))]
        out_shapes += [jax.ShapeDtypeStruct((m, n_out), _BF16), jax.ShapeDtypeStruct((m, 1), _F32)]
        blocks += [bm * bn * 2, bm * _LANES * 4]
        scratch.append(pltpu.VMEM((bm, 1), _F32))
    if has_meta:
        a_meta, r_meta = meta
        in_specs.append(pl.BlockSpec((_N_META, k), lambda i, j: (0, 0)))
        args.append(a_meta)
        blocks.append(_N_META * k * 2)
        if has_resid:
            in_specs.append(pl.BlockSpec((_N_META, bn), lambda i, j: (0, j)))
            args.append(r_meta)
        if has_scale:
            in_specs.append(pl.BlockSpec((_N_META, 1), lambda i, j: (0, 0)))
            args.append(scale[1])
        out_specs.append(pl.BlockSpec((1, _N_META, bn), lambda i, j: (i, 0, j)))
        out_shapes.append(jax.ShapeDtypeStruct((n_i, _N_META, n_out), out_dtype))
    j_in, j_args, j_out, j_shapes, j_blocks, j_desc = _job_plumbing(jobs, n_i * n_j, lambda i, j: i * n_j + j)

    kernel = functools.partial(_matmul_kernel, w_transposed=w_transposed, epilogue=epilogue, has_resid=has_resid,
                               has_scale=has_scale, has_meta=has_meta, emit_stats=emit_stats, jobs=j_desc,
                               n_inner=n_j, n_cols=n_out, row_split=row_split)
    outs = pl.pallas_call(
        kernel,
        grid=(n_i, n_j),
        in_specs=in_specs + j_in,
        out_specs=out_specs + j_out,
        out_shape=out_shapes + j_shapes,
        scratch_shapes=scratch,
        compiler_params=_params(blocks + j_blocks,
                                scratch_bytes=(len(ws) + 1 + emit_stats) * (bm // row_split) * bn * 4),
        name=name,
    )(*args, *j_args)
    n_main = 3 if emit_stats else 1
    main = list(outs[:n_main])
    out_meta = outs[n_main][0] if has_meta else None
    return (*main, out_meta, list(outs[n_main + has_meta:]))


def _cumsum_rows(tri, g):
    g1 = g.astype(_BF16)
    r1 = g - g1.astype(_F32)
    g2 = r1.astype(_BF16)
    g3 = (r1 - g2.astype(_F32)).astype(_BF16)
    out = jnp.dot(tri, g1, preferred_element_type=_F32)
    out = out + jnp.dot(tri, g2, preferred_element_type=_F32)
    return out + jnp.dot(tri, g3, preferred_element_type=_F32)


def _gla_kernel(q_ref, k_ref, v_ref, r_ref, al_ref, wa2_ref, ba_ref, hnorm_ref, s0_ref, o_ref, sfin_ref,
                s_scr, b_scr, bnext_scr, o_scr, *, chunk, n_chunks, sub):
    step = pl.program_id(2)

    @pl.when(step == 0)
    def _load_state():
        s_scr[...] = s0_ref[0]

    c = chunk
    state_tile = 2 * _LANES
    n_sub = c // sub
    tri = (lax.broadcasted_iota(jnp.int32, (c, c), 0) >= lax.broadcasted_iota(jnp.int32, (c, c), 1)).astype(_BF16)
    sub_row = lax.broadcasted_iota(jnp.int32, (sub, c), 0)
    sub_col = lax.broadcasted_iota(jnp.int32, (sub, c), 1)

    q_scale = _DK ** -0.5
    trans_b = (((1,), (1,)), ((), ()))

    def gate_of_chunk(ci):
        rows = pl.ds(ci * c if isinstance(ci, int) else pl.multiple_of(ci * c, c), c)
        return jnp.dot(al_ref[rows, :].astype(_BF16), wa2_ref[...], preferred_element_type=_F32) + ba_ref[...]

    def decay_of_gate(x):
        g = (jnp.minimum(x, 0.0) - jnp.log(1.0 + jnp.exp(-jnp.abs(x)))) * (_LOG2E / _GATE_TAU)
        return _cumsum_rows(tri, g)

    def one_chunk(ci, carry):
        base = pl.multiple_of(ci * c, c)
        rows = pl.ds(base, c)
        b_scr[...] = bnext_scr[...]
        b_last = b_scr[c - 1:c, :]
        x_next = gate_of_chunk(jnp.minimum(ci + 1, n_chunks - 1))

        q_dec = (q_ref[rows, :].astype(_F32) * q_scale * jnp.exp2(b_scr[...])).astype(_BF16)
        o_scr[...] = jnp.dot(q_dec, s_scr[...].astype(_BF16), preferred_element_type=_F32)

        diag, off = [], [None]
        for blk in range(n_sub):
            r0 = blk * sub
            b_b = b_scr[r0:r0 + sub, :]
            q_b = q_ref[pl.ds(base + r0, sub), :].astype(_F32) * q_scale
            stacked = jnp.concatenate(
                [q_b * jnp.exp2(b_b - b_scr[r0 + j:r0 + j + 1, :]) for j in range(sub)], axis=0).astype(_BF16)
            diag.append(lax.dot_general(stacked, k_ref[rows, :], trans_b, preferred_element_type=_F32))
            if blk > 0:
                p = b_scr[r0 - 1:r0, :]
                q_o = (q_b * jnp.exp2(b_b - p)).astype(_BF16)
                k_o = (k_ref[rows, :].astype(_F32) * jnp.exp2(p - b_scr[...])).astype(_BF16)
                off.append(lax.dot_general(q_o, k_o, trans_b, preferred_element_type=_F32))

        b_next = decay_of_gate(x_next)

        tiles = []
        for blk in range(n_sub):
            r0 = blk * sub
            sc = jnp.zeros((sub, c), _F32)
            for j in range(sub):
                sc = jnp.where(sub_col == r0 + j, diag[blk][j * sub:(j + 1) * sub], sc)
            sc = jnp.where(sub_row + r0 >= sub_col, sc, 0.0)
            if blk > 0:
                sc = sc + jnp.where(sub_col < r0, off[blk], 0.0)
            tiles.append(sc)
        scores = jnp.concatenate(tiles, axis=0) if n_sub > 1 else tiles[0]
        o_intra = jnp.dot(scores.astype(_BF16), v_ref[rows, :], preferred_element_type=_F32)
        bnext_scr[...] = b_next

        k_dec = (k_ref[rows, :].astype(_F32) * jnp.exp2(b_last - b_scr[...])).astype(_BF16)
        decay_col = jnp.transpose(jnp.broadcast_to(jnp.exp2(b_last), (_LANES, _DK)))
        for t in range(_DV // state_tile):
            cols = slice(t * state_tile, (t + 1) * state_tile)
            upd = lax.dot_general(k_dec, v_ref[rows, cols], (((0,), (0,)), ((), ())), preferred_element_type=_F32)
            s_scr[:, cols] = s_scr[:, cols] * jnp.tile(decay_col, (1, state_tile // _LANES)) + upd

        o = o_scr[...] + o_intra
        ms = jnp.mean(o * o, axis=-1, keepdims=True)
        o_n = o * lax.rsqrt(ms + _EPS) * hnorm_ref[...]
        r = r_ref[rows, :].astype(_F32)
        o_ref[rows, :] = (o_n * (r * jax.nn.sigmoid(r))).astype(o_ref.dtype)
        return carry

    bnext_scr[...] = decay_of_gate(gate_of_chunk(0))
    lax.fori_loop(0, n_chunks, one_chunk, 0)

    @pl.when(step == pl.num_programs(2) - 1)
    def _store_state():
        sfin_ref[0, 0] = s_scr[...]


def _gla(proj, a_low, wa2, ba, hnorm, s0, *, n_batch, seq, chunk, rows_per_step):
    m = proj.shape[0]
    assert m == n_batch * seq and seq % rows_per_step == 0 and rows_per_step % chunk == 0
    steps = seq // rows_per_step
    r = rows_per_step
    kv = _QK // _DV

    def row(b, h, i):
        return b * steps + i

    kernel = functools.partial(_gla_kernel, chunk=chunk, n_chunks=r // chunk, sub=min(chunk, 8))
    blocks = [r * _DK * 2, r * _DK * 2, r * _DV * 2, r * _DV * 2, r * _LANES * 4, _LANES * _DK * 2,
              _DK * 4, _DV * 4, _DK * _DV * 4, r * _DV * 2, _DK * _DV * 4]
    outs = pl.pallas_call(
        kernel,
        grid=(n_batch, _HEADS, steps),
        in_specs=[
            pl.BlockSpec((r, _DK), lambda b, h, i: (row(b, h, i), h)),
            pl.BlockSpec((r, _DK), lambda b, h, i: (row(b, h, i), _HEADS + h)),
            pl.BlockSpec((r, _DV), lambda b, h, i: (row(b, h, i), 2 * kv + h)),
            pl.BlockSpec((r, _DV), lambda b, h, i: (row(b, h, i), 2 * kv + _HEADS + h)),
            pl.BlockSpec((r, _LANES), lambda b, h, i: (row(b, h, i), 0)),
            pl.BlockSpec((_LANES, _DK), lambda b, h, i: (0, h)),
            pl.BlockSpec((1, _DK), lambda b, h, i: (0, h)),
            pl.BlockSpec((1, _DV), lambda b, h, i: (0, 0)),
            pl.BlockSpec((1, _DK, _DV), lambda b, h, i: (h, 0, 0)),
        ],
        out_specs=[
            pl.BlockSpec((r, _DV), lambda b, h, i: (row(b, h, i), h)),
            pl.BlockSpec((1, 1, _DK, _DV), lambda b, h, i: (b, h, 0, 0)),
        ],
        out_shape=[
            jax.ShapeDtypeStruct((m, _V), _BF16),
            jax.ShapeDtypeStruct((n_batch, _HEADS, _DK, _DV), _F32),
        ],
        scratch_shapes=[pltpu.VMEM((_DK, _DV), _F32), pltpu.VMEM((chunk, _DK), _F32), pltpu.VMEM((chunk, _DK), _F32),
                        pltpu.VMEM((chunk, _DV), _F32)],
        compiler_params=_params(blocks, scratch_bytes=4 * _DK * _DV * 4, n_grid=3),
        name="gla",
    )(proj, proj, proj, proj, a_low, wa2, ba.reshape(1, _QK), hnorm.reshape(1, _DV), s0)
    return outs[0], outs[1]


def _conv_in_kernel(a_ref, wb_ref, wc_ref, wu_ref, s_ref, am_ref, sm_ref, cw_ref, o_ref, tail_scr, meta_scr,
                    *, blocks_per_seq, row_split):
    i, j = pl.program_id(0), pl.program_id(1)

    @pl.when(i == 0)
    def _meta_rows():
        am, sm = am_ref[...], sm_ref[...]
        z_meta = ((jnp.dot(am, wc_ref[0], preferred_element_type=_F32) * sm)
                  * (jnp.dot(am, wu_ref[0], preferred_element_type=_F32) * sm))
        meta_scr[j] = z_meta[_N_META - _SUBLANES:]
        tail_scr[j] = z_meta[_N_META - _SUBLANES:]

    prev = jnp.where(i % blocks_per_seq == 0, meta_scr[j], tail_scr[j])
    cw = cw_ref[...]
    sub_rows = o_ref.shape[0] // row_split
    rid = lax.broadcasted_iota(jnp.int32, (sub_rows, o_ref.shape[1]), 0)
    for h in range(row_split):
        rows = slice(h * sub_rows, (h + 1) * sub_rows)
        a, s = a_ref[rows, :], s_ref[rows, :]
        bg = jnp.dot(a, wb_ref[0], preferred_element_type=_F32) * s
        z = ((jnp.dot(a, wc_ref[0], preferred_element_type=_F32) * s)
             * (jnp.dot(a, wu_ref[0], preferred_element_type=_F32) * s))
        z_m1, z_m2 = prev[_SUBLANES - 1:_SUBLANES], prev[_SUBLANES - 2:_SUBLANES - 1]
        z1 = jnp.where(rid == 0, z_m1, pltpu.roll(z, 1, 0))
        z2 = jnp.where(rid == 0, z_m2, jnp.where(rid == 1, z_m1, pltpu.roll(z, 2, 0)))
        conv = cw[2:3] * z + cw[0:1] * z2 + cw[1:2] * z1
        o_ref[rows, :] = (bg * conv).astype(o_ref.dtype)
        prev = z[sub_rows - _SUBLANES:]
    tail_scr[j] = prev


def _conv_in(a, rstd, a_meta, rstd_meta, w, conv_w, *, seq, bm, bn, row_split):
    m, k = a.shape
    assert m % bm == 0 and seq % bm == 0 and _D % bn == 0
    n_i, n_j = m // bm, _D // bn
    kernel = functools.partial(_conv_in_kernel, blocks_per_seq=seq // bm, row_split=row_split)
    blocks = [bm * k * 2, 3 * k * bn * 2, bm * _LANES * 4, _N_META * k * 2, _CONV_WIDTH * bn * 4, bm * bn * 2]
    return pl.pallas_call(
        kernel,
        grid=(n_i, n_j),
        in_specs=[
            pl.BlockSpec((bm, k), lambda i, j: (i, 0)),
            pl.BlockSpec((1, k, bn), lambda i, j: (j, 0, 0)),
            pl.BlockSpec((1, k, bn), lambda i, j: (n_j + j, 0, 0)),
            pl.BlockSpec((1, k, bn), lambda i, j: (2 * n_j + j, 0, 0)),
            pl.BlockSpec((bm, 1), lambda i, j: (i, 0)),
            pl.BlockSpec((_N_META, k), lambda i, j: (0, 0)),
            pl.BlockSpec((_N_META, 1), lambda i, j: (0, 0)),
            pl.BlockSpec((_CONV_WIDTH, bn), lambda i, j: (0, j)),
        ],
        out_specs=pl.BlockSpec((bm, bn), lambda i, j: (i, j)),
        out_shape=jax.ShapeDtypeStruct((m, _D), _BF16),
        scratch_shapes=[pltpu.VMEM((n_j, _SUBLANES, bn), _F32), pltpu.VMEM((n_j, _SUBLANES, bn), _F32)],
        compiler_params=_params(blocks, scratch_bytes=8 * (bm // row_split) * bn * 4),
        name="conv_in",
    )(a, w, w, w, rstd, a_meta, rstd_meta, conv_w)


def kernel(x, meta, norm_mix, norm_ffn, gla_w_in, gla_w_a2, gla_b_a, gla_head_norm, gla_w_out,
           conv_w_in, conv_w, conv_w_out, ffn_w_gate, ffn_w_up, ffn_w_down, norm_final):
    n_batch, seq, d = x.shape
    xm = x.reshape(n_batch * seq, d)
    xmeta = meta.astype(x.dtype)
    n_qkvr = 2 * _QK + 2 * _V
    d_ff = ffn_w_gate.shape[2]
    assert d_ff % 256 == 0


    w_in_t = jnp.transpose(gla_w_in, (0, 2, 1))
    w_qkvr = _cast_rows(w_in_t, 0, norm_mix[0], 0, n_qkvr, tile=1024, rb=512)
    w_a1 = _cast_rows(w_in_t, 0, norm_mix[0], n_qkvr, _RANK, tile=_LANES, rb=_LANES)
    w_a2 = jnp.pad(gla_w_a2[0], ((0, _LANES - _RANK), (0, 0))).astype(_BF16)

    hb, rstd = _row_stats(xm)
    hb_meta, rstd_meta = _row_stats(xmeta)
    proj, proj_meta, (w_out0, w_gate0, w_up0) = _matmul(
        hb, [(w_qkvr, 0)], n_qkvr, _epilogue_cast, _BF16, bm=1024, bn=1024, scale=(rstd, rstd_meta),
        meta=(hb_meta, None),
        jobs=[_CastJob(gla_w_out, 0, 64, 512), _CastJob(ffn_w_gate, 0, 128, 256, norm_ffn[0]),
              _CastJob(ffn_w_up, 0, 128, 256, norm_ffn[0])],
        row_split=4, name="gla_in")
    a_low, a_low_meta, _ = _matmul(hb, [(w_a1, 0)], _LANES, _epilogue_cast, _F32, bm=1024, bn=_LANES,
                                   scale=(rstd, rstd_meta), meta=(hb_meta, None), name="gla_gate_in")

    zero_state = jnp.zeros((_HEADS, _DK, _DV), _F32)
    o_meta, s_meta = _gla(proj_meta, a_low_meta, w_a2, gla_b_a[0], gla_head_norm[0], zero_state,
                          n_batch=1, seq=_N_META, chunk=_N_META, rows_per_step=_N_META)
    o_x, _ = _gla(proj, a_low, w_a2, gla_b_a[0], gla_head_norm[0], s_meta[0],
                  n_batch=n_batch, seq=seq, chunk=64, rows_per_step=512)
    h, hb, rstd, h_meta, (w_down0,) = _matmul(
        o_x, [(w_out0, 0)], d, _epilogue_resid, _F32, bm=1024, bn=512, resid=xm, meta=(o_meta, xmeta),
        emit_stats=True, jobs=[_CastJob(ffn_w_down, 0, d_ff // 128, 512)], row_split=4, name="gla_out")

    hb_meta, rstd_meta = _row_stats(h_meta)
    act, act_meta, (w_cout, w_gate1, w_up1, w_cin) = _matmul(
        hb, [(w_gate0, 0), (w_up0, 0)], d_ff, _epilogue_swiglu, _BF16, bm=2048, bn=256, scale=(rstd, rstd_meta),
        meta=(hb_meta, None),
        jobs=[_CastJob(conv_w_out, 0, 64, 512), _CastJob(ffn_w_gate, 1, 256, 256, norm_ffn[1]),
              _CastJob(ffn_w_up, 1, 256, 256, norm_ffn[1]), _CastJob(conv_w_in, 0, 256, 256, norm_mix[1])],
        row_split=8, name="ffn_gate_up")
    h, hb, rstd, h_meta, (w_down1,) = _matmul(
        act, [(w_down0, 0)], d, _epilogue_resid, _F32, bm=512, bn=512, resid=h, meta=(act_meta, h_meta),
        emit_stats=True, jobs=[_CastJob(ffn_w_down, 1, d_ff // 64, 512)], row_split=2, name="ffn_down")

    hb_meta, rstd_meta = _row_stats(h_meta)
    y = _conv_in(hb, rstd, hb_meta, rstd_meta, w_cin, conv_w[0], seq=seq, bm=2048, bn=256, row_split=8)
    h, hb, rstd, _, _ = _matmul(y, [(w_cout, 0)], d, _epilogue_resid, _F32, bm=1024, bn=512, resid=h,
                                emit_stats=True, row_split=4, name="conv_out")

    act, _, _ = _matmul(hb, [(w_gate1, 0), (w_up1, 0)], d_ff, _epilogue_swiglu, _BF16, bm=2048, bn=256,
                        scale=(rstd, None), row_split=8, name="ffn_gate_up")
    h, _, _ = _matmul(act, [(w_down1, 0)], d, _epilogue_resid, _F32, bm=512, bn=512, resid=h, row_split=2,
                      name="ffn_down")
    out = _rmsnorm(h, norm_final, _F32)
    return out.reshape(n_batch, seq, d)
```

```python
import functools
from typing import NamedTuple

import jax
import jax.numpy as jnp
from jax import lax
from jax.experimental import pallas as pl
from jax.experimental.pallas import tpu as pltpu

_D = 4096
_N_META = 16
_HEADS = 4
_DK = 512
_DV = 1024
_QK = _HEADS * _DK
_V = _HEADS * _DV
_RANK = 16
_GATE_TAU = 16.0
_CONV_WIDTH = 3
_EPS = 1e-6
_LOG2E = 1.4426950408889634

_LANES = 128
_SUBLANES = 8
_BF16_SUBLANES = 16
_V7X_SCOPED_VMEM_BYTES = 60000 * 1024

_F32 = jnp.float32
_BF16 = jnp.bfloat16


def _params(block_bytes, scratch_bytes=0, n_grid=2):
    need = 2 * sum(block_bytes) + scratch_bytes
    assert need <= _V7X_SCOPED_VMEM_BYTES, (need, _V7X_SCOPED_VMEM_BYTES)
    return pltpu.CompilerParams(
        dimension_semantics=("arbitrary",) * n_grid,
        vmem_limit_bytes=_V7X_SCOPED_VMEM_BYTES,
    )


class _CastJob(NamedTuple):
    src: jax.Array
    layer: int
    n_blocks: int
    tile_cols: int
    gain: jax.Array = None

    @property
    def block(self):
        rows, cols = self.src.shape[1:]
        assert rows % self.n_blocks == 0 and (rows // self.n_blocks) % _BF16_SUBLANES == 0
        assert cols % self.tile_cols == 0
        return rows // self.n_blocks, cols


def _job_plumbing(jobs, n_steps, flat_step):
    in_specs, args, out_specs, out_shapes, blocks, desc = [], [], [], [], [], []
    for job in jobs:
        assert job.n_blocks <= n_steps, (job.n_blocks, n_steps)
        rb, cols = job.block
        last, layer = job.n_blocks - 1, job.layer

        def blk(*g, last=last):
            return jnp.minimum(flat_step(*g), last)

        in_specs.append(pl.BlockSpec((1, rb, cols), lambda *g, blk=blk, layer=layer: (layer, blk(*g), 0)))
        args.append(job.src)
        if job.gain is not None:
            in_specs.append(pl.BlockSpec((rb, 1), lambda *g, blk=blk: (blk(*g), 0)))
            args.append(job.gain.reshape(-1, 1))
        n_tiles, tc = cols // job.tile_cols, job.tile_cols
        out_specs.append(pl.BlockSpec((n_tiles, rb, tc), lambda *g, blk=blk: (0, blk(*g), 0)))
        out_shapes.append(jax.ShapeDtypeStruct((n_tiles, job.src.shape[1], tc), _BF16))
        blocks += [rb * cols * 4, rb * cols * 2, rb * _LANES * 4]
        desc.append(job.gain is not None)
    return in_specs, args, out_specs, out_shapes, blocks, tuple(desc)


def _take_job_inputs(it, desc):
    return [(next(it), next(it) if has_gain else None) for has_gain in desc]


def _run_casts(in_refs, dst_refs):
    for (src, gain), dst in zip(in_refs, dst_refs):
        w = src[0]
        if gain is not None:
            w = w * gain[...]
        w = w.astype(_BF16)
        n_tiles, _, tc = dst.shape
        for t in range(n_tiles):
            dst[t] = w[:, t * tc:(t + 1) * tc]


def _cast_rows_kernel(src_ref, gain_ref, dst_ref, *, n_valid):
    w = src_ref[0] * gain_ref[...]
    if n_valid < w.shape[0]:
        w = jnp.where(lax.broadcasted_iota(jnp.int32, w.shape, 0) < n_valid, w, 0.0)
    dst_ref[0] = jnp.transpose(w).astype(_BF16)


def _cast_rows(src_t, layer, gain, row0, n_rows, tile, rb):
    src_rows, cols = src_t.shape[1:]
    n_valid = min(n_rows, src_rows - row0)
    n_tiles = -(-n_rows // tile)
    per_tile = tile // rb
    assert tile % rb == 0 and row0 % rb == 0 and (n_valid == n_tiles * tile or (n_tiles == 1 and rb == tile))
    r0 = row0 // rb
    return pl.pallas_call(
        functools.partial(_cast_rows_kernel, n_valid=n_valid),
        grid=(n_tiles * per_tile,),
        in_specs=[pl.BlockSpec((1, rb, cols), lambda i: (layer, r0 + i, 0)),
                  pl.BlockSpec((1, cols), lambda i: (0, 0))],
        out_specs=pl.BlockSpec((1, cols, rb), lambda i: (i // per_tile, 0, i % per_tile)),
        out_shape=jax.ShapeDtypeStruct((n_tiles, cols, tile), _BF16),
        compiler_params=_params([rb * cols * 4, rb * cols * 2, cols * 4], scratch_bytes=2 * rb * cols * 4, n_grid=1),
        name="weight_cast",
    )(src_t, gain.reshape(1, -1))


def _row_stats_kernel(x_ref, xb_ref, rstd_ref):
    x = x_ref[...]
    xb_ref[...] = x.astype(_BF16)
    rstd_ref[...] = lax.rsqrt(jnp.mean(x * x, axis=-1, keepdims=True) + _EPS)


def _row_stats(x, bm=256):
    m, d = x.shape
    bm = min(bm, m)
    assert m % bm == 0
    return pl.pallas_call(
        _row_stats_kernel,
        grid=(m // bm,),
        in_specs=[pl.BlockSpec((bm, d), lambda i: (i, 0))],
        out_specs=[pl.BlockSpec((bm, d), lambda i: (i, 0)), pl.BlockSpec((bm, 1), lambda i: (i, 0))],
        out_shape=[jax.ShapeDtypeStruct((m, d), _BF16), jax.ShapeDtypeStruct((m, 1), _F32)],
        compiler_params=_params([bm * d * 4, bm * d * 2, bm * _LANES * 4], scratch_bytes=2 * bm * d * 4, n_grid=1),
        name="row_stats",
    )(x)


def _rmsnorm_kernel(x_ref, g_ref, o_ref):
    x = x_ref[...]
    ms = jnp.mean(x * x, axis=-1, keepdims=True)
    o_ref[...] = (x * lax.rsqrt(ms + _EPS) * g_ref[...]).astype(o_ref.dtype)


def _rmsnorm(x, g, out_dtype, bm=256):
    m, d = x.shape
    bm = min(bm, m)
    assert m % bm == 0
    blocks = [bm * d * 4, d * 4, bm * d * jnp.dtype(out_dtype).itemsize]
    return pl.pallas_call(
        _rmsnorm_kernel,
        grid=(m // bm,),
        in_specs=[pl.BlockSpec((bm, d), lambda i: (i, 0)), pl.BlockSpec((1, d), lambda i: (0, 0))],
        out_specs=pl.BlockSpec((bm, d), lambda i: (i, 0)),
        out_shape=jax.ShapeDtypeStruct((m, d), out_dtype),
        compiler_params=_params(blocks, scratch_bytes=2 * bm * d * 4, n_grid=1),
        name="rmsnorm",
    )(x, g.reshape(1, d))


def _epilogue_cast(accs, resid):
    return accs[0]


def _epilogue_resid(accs, resid):
    return resid + accs[0]


def _epilogue_swiglu(accs, resid):
    gate, up = accs
    return gate * jax.nn.sigmoid(gate) * up


def _matmul_kernel(*refs, w_transposed, epilogue, has_resid, has_scale, has_meta, emit_stats, jobs, n_cols,
                   row_split):
    it = iter(refs)
    a_ref = next(it)
    w_refs = [next(it) for _ in w_transposed]
    r_ref = next(it) if has_resid else None
    s_ref = next(it) if has_scale else None
    am_ref = next(it) if has_meta else None
    rm_ref = next(it) if has_meta and has_resid else None
    sm_ref = next(it) if has_meta and has_scale else None
    job_in = _take_job_inputs(it, jobs)
    o_ref = next(it)
    ob_ref, rstd_ref = (next(it), next(it)) if emit_stats else (None, None)
    om_ref = next(it) if has_meta else None
    dst_refs = [next(it) for _ in jobs]
    ssq_scr = next(it) if emit_stats else None
    i, j = pl.program_id(0), pl.program_id(1)

    def result(a, resid, scale):
        accs = [lax.dot_general(a, w[0], (((1,), (1 if t else 0,)), ((), ())), preferred_element_type=_F32)
                for w, t in zip(w_refs, w_transposed)]
        if scale is not None:
            accs = [acc * scale for acc in accs]
        return epilogue(accs, resid)

    if emit_stats:
        @pl.when(j == 0)
        def _reset():
            ssq_scr[...] = jnp.zeros(ssq_scr.shape, _F32)

    sub_rows = o_ref.shape[0] // row_split
    for h in range(row_split):
        rows = slice(h * sub_rows, (h + 1) * sub_rows)
        out = result(a_ref[rows, :], r_ref[rows, :] if has_resid else None, s_ref[rows, :] if has_scale else None)
        o_ref[rows, :] = out.astype(o_ref.dtype)
        if emit_stats:
            ob_ref[rows, :] = out.astype(_BF16)
            ssq = ssq_scr[rows, :] + jnp.sum(out * out, axis=-1, keepdims=True)
            ssq_scr[rows, :] = ssq
            rstd_ref[rows, :] = lax.rsqrt(ssq * (1.0 / n_cols) + _EPS)
        _run_casts(job_in[h::row_split], dst_refs[h::row_split])

    if has_meta:
        @pl.when(i == 0)
        def _meta_rows():
            out_m = result(am_ref[...], rm_ref[...] if has_resid else None, sm_ref[...] if has_scale else None)
            om_ref[0] = out_m.astype(om_ref.dtype)

        @pl.when(i != 0)
        def _meta_unused():
            om_ref[0] = jnp.zeros(om_ref.shape[1:], om_ref.dtype)


def _matmul(a, ws, n_out, epilogue, out_dtype, bm, bn, resid=None, scale=None, meta=None, emit_stats=False,
            jobs=(), row_split=1, name="matmul"):
    m, k = a.shape
    bm = min(bm, m)
    assert m % bm == 0 and n_out % bn == 0
    n_i, n_j = m // bm, n_out // bn
    out_bytes = jnp.dtype(out_dtype).itemsize
    has_resid, has_scale, has_meta = resid is not None, scale is not None, meta is not None

    in_specs = [pl.BlockSpec((bm, k), lambda i, j: (i, 0))]
    args = [a]
    blocks = [bm * k * 2, bm * bn * out_bytes]
    w_transposed = tuple(len(entry) > 2 and entry[2] for entry in ws)
    for (w, off, *_), transposed in zip(ws, w_transposed):
        tile = (bn, k) if transposed else (k, bn)
        assert w.shape[1:] == tile, (w.shape, tile)
        in_specs.append(pl.BlockSpec((1, *tile), lambda i, j, off=off: (off + j, 0, 0)))
        args.append(w)
        blocks.append(k * bn * 2)
    if has_resid:
        in_specs.append(pl.BlockSpec((bm, bn), lambda i, j: (i, j)))
        args.append(resid)
        blocks.append(bm * bn * 4)
    if has_scale:
        in_specs.append(pl.BlockSpec((bm, 1), lambda i, j: (i, 0)))
        args.append(scale[0])
        blocks.append(bm * _LANES * 4)
    out_specs = [pl.BlockSpec((bm, bn), lambda i, j: (i, j))]
    out_shapes = [jax.ShapeDtypeStruct((m, n_out), out_dtype)]
    scratch = []
    if emit_stats:
        out_specs += [pl.BlockSpec((bm, bn), lambda i, j: (i, j)), pl.BlockSpec((bm, 1), lambda i, j: (i, 0))]
        out_shapes += [jax.ShapeDtypeStruct((m, n_out), _BF16), jax.ShapeDtypeStruct((m, 1), _F32)]
        blocks += [bm * bn * 2, bm * _LANES * 4]
        scratch.append(pltpu.VMEM((bm, 1), _F32))
    if has_meta:
        a_meta, r_meta = meta
        in_specs.append(pl.BlockSpec((_N_META, k), lambda i, j: (0, 0)))
        args.append(a_meta)
        blocks.append(_N_META * k * 2)
        if has_resid:
            in_specs.append(pl.BlockSpec((_N_META, bn), lambda i, j: (0, j)))
            args.append(r_meta)
        if has_scale:
            in_specs.append(pl.BlockSpec((_N_META, 1), lambda i, j: (0, 0)))
            args.append(scale[1])
        out_specs.append(pl.BlockSpec((1, _N_META, bn), lambda i, j: (i, 0, j)))
        out_shapes.append(jax.ShapeDtypeStruct((n_i, _N_META, n_out), out_dtype))
    j_in, j_args, j_out, j_shapes, j_blocks, j_desc = _job_plumbing(jobs, n_i * n_j, lambda i, j: i * n_j + j)

    kernel = functools.partial(_matmul_kernel, w_transposed=w_transposed, epilogue=epilogue, has_resid=has_resid,
                               has_scale=has_scale, has_meta=has_meta, emit_stats=emit_stats, jobs=j_desc,
                               n_cols=n_out, row_split=row_split)
    outs = pl.pallas_call(
        kernel,
        grid=(n_i, n_j),
        in_specs=in_specs + j_in,
        out_specs=out_specs + j_out,
        out_shape=out_shapes + j_shapes,
        scratch_shapes=scratch,
        compiler_params=_params(blocks + j_blocks,
                                scratch_bytes=(len(ws) + 1 + emit_stats) * (bm // row_split) * bn * 4),
        name=name,
    )(*args, *j_args)
    n_main = 3 if emit_stats else 1
    main = list(outs[:n_main])
    out_meta = outs[n_main][0] if has_meta else None
    return (*main, out_meta, list(outs[n_main + has_meta:]))


def _cumsum_rows(tri, g):
    g1 = g.astype(_BF16)
    r1 = g - g1.astype(_F32)
    g2 = r1.astype(_BF16)
    g3 = (r1 - g2.astype(_F32)).astype(_BF16)
    out = jnp.dot(tri, g1, preferred_element_type=_F32)
    out = out + jnp.dot(tri, g2, preferred_element_type=_F32)
    return out + jnp.dot(tri, g3, preferred_element_type=_F32)


def _gla_kernel(q_ref, k_ref, v_ref, r_ref, al_ref, wa2_ref, ba_ref, hnorm_ref, s0_ref, o_ref, sfin_ref,
                s_scr, b_scr, bnext_scr, o_scr, *, chunk, n_chunks, sub):
    step = pl.program_id(2)

    @pl.when(step == 0)
    def _load_state():
        s_scr[...] = s0_ref[0]

    c = chunk
    state_tile = 2 * _LANES
    n_sub = c // sub
    tri = (lax.broadcasted_iota(jnp.int32, (c, c), 0) >= lax.broadcasted_iota(jnp.int32, (c, c), 1)).astype(_BF16)
    sub_row = lax.broadcasted_iota(jnp.int32, (sub, c), 0)
    sub_col = lax.broadcasted_iota(jnp.int32, (sub, c), 1)

    q_scale = _DK ** -0.5
    trans_b = (((1,), (1,)), ((), ()))

    def gate_of_chunk(ci):
        rows = pl.ds(ci * c if isinstance(ci, int) else pl.multiple_of(ci * c, c), c)
        return jnp.dot(al_ref[rows, :].astype(_BF16), wa2_ref[...], preferred_element_type=_F32) + ba_ref[...]

    def decay_of_gate(x):
        g = (jnp.minimum(x, 0.0) - jnp.log(1.0 + jnp.exp(-jnp.abs(x)))) * (_LOG2E / _GATE_TAU)
        return _cumsum_rows(tri, g)

    def one_chunk(ci, carry):
        base = pl.multiple_of(ci * c, c)
        rows = pl.ds(base, c)
        b_scr[...] = bnext_scr[...]
        b_last = b_scr[c - 1:c, :]
        x_next = gate_of_chunk(jnp.minimum(ci + 1, n_chunks - 1))

        q_dec = (q_ref[rows, :].astype(_F32) * q_scale * jnp.exp2(b_scr[...])).astype(_BF16)
        o_scr[...] = jnp.dot(q_dec, s_scr[...].astype(_BF16), preferred_element_type=_F32)

        diag, off = [], [None]
        for blk in range(n_sub):
            r0 = blk * sub
            b_b = b_scr[r0:r0 + sub, :]
            q_b = q_ref[pl.ds(base + r0, sub), :].astype(_F32) * q_scale
            stacked = jnp.concatenate(
                [q_b * jnp.exp2(b_b - b_scr[r0 + j:r0 + j + 1, :]) for j in range(sub)], axis=0).astype(_BF16)
            diag.append(lax.dot_general(stacked, k_ref[rows, :], trans_b, preferred_element_type=_F32))
            if blk > 0:
                p = b_scr[r0 - 1:r0, :]
                q_o = (q_b * jnp.exp2(b_b - p)).astype(_BF16)
                n_keys = -(-r0 // _BF16_SUBLANES) * _BF16_SUBLANES
                k_o = (k_ref[pl.ds(base, n_keys), :].astype(_F32) * jnp.exp2(p - b_scr[0:n_keys, :])).astype(_BF16)
                if n_keys < c:
                    k_o = jnp.concatenate([k_o, jnp.zeros((c - n_keys, _DK), _BF16)], axis=0)
                off.append(lax.dot_general(q_o, k_o, trans_b, preferred_element_type=_F32))

        b_next = decay_of_gate(x_next)

        tiles = []
        for blk in range(n_sub):
            r0 = blk * sub
            sc = jnp.zeros((sub, c), _F32)
            for j in range(sub):
                sc = jnp.where(sub_col == r0 + j, diag[blk][j * sub:(j + 1) * sub], sc)
            sc = jnp.where(sub_row + r0 >= sub_col, sc, 0.0)
            if blk > 0:
                sc = sc + jnp.where(sub_col < r0, off[blk], 0.0)
            tiles.append(sc)
        scores = jnp.concatenate(tiles, axis=0) if n_sub > 1 else tiles[0]
        o_intra = jnp.dot(scores.astype(_BF16), v_ref[rows, :], preferred_element_type=_F32)
        bnext_scr[...] = b_next

        k_dec = (k_ref[rows, :].astype(_F32) * jnp.exp2(b_last - b_scr[...])).astype(_BF16)
        decay_col = jnp.transpose(jnp.broadcast_to(jnp.exp2(b_last), (_LANES, _DK)))
        for t in range(_DV // state_tile):
            cols = slice(t * state_tile, (t + 1) * state_tile)
            upd = lax.dot_general(k_dec, v_ref[rows, cols], (((0,), (0,)), ((), ())), preferred_element_type=_F32)
            s_scr[:, cols] = s_scr[:, cols] * jnp.tile(decay_col, (1, state_tile // _LANES)) + upd

        o = o_scr[...] + o_intra
        ms = jnp.mean(o * o, axis=-1, keepdims=True)
        o_n = o * lax.rsqrt(ms + _EPS) * hnorm_ref[...]
        r = r_ref[rows, :].astype(_F32)
        o_ref[rows, :] = (o_n * (r * jax.nn.sigmoid(r))).astype(o_ref.dtype)
        return carry

    bnext_scr[...] = decay_of_gate(gate_of_chunk(0))
    lax.fori_loop(0, n_chunks, one_chunk, 0)

    @pl.when(step == pl.num_programs(2) - 1)
    def _store_state():
        sfin_ref[0, 0] = s_scr[...]


def _gla(proj, a_low, wa2, ba, hnorm, s0, *, n_batch, seq, chunk, rows_per_step):
    m = proj.shape[0]
    assert m == n_batch * seq and seq % rows_per_step == 0 and rows_per_step % chunk == 0
    steps = seq // rows_per_step
    r = rows_per_step
    kv = _QK // _DV

    def row(b, h, i):
        return b * steps + i

    kernel = functools.partial(_gla_kernel, chunk=chunk, n_chunks=r // chunk, sub=min(chunk, 8))
    blocks = [r * _DK * 2, r * _DK * 2, r * _DV * 2, r * _DV * 2, r * _LANES * 4, _LANES * _DK * 2,
              _DK * 4, _DV * 4, _DK * _DV * 4, r * _DV * 2, _DK * _DV * 4]
    outs = pl.pallas_call(
        kernel,
        grid=(n_batch, _HEADS, steps),
        in_specs=[
            pl.BlockSpec((r, _DK), lambda b, h, i: (row(b, h, i), h)),
            pl.BlockSpec((r, _DK), lambda b, h, i: (row(b, h, i), _HEADS + h)),
            pl.BlockSpec((r, _DV), lambda b, h, i: (row(b, h, i), 2 * kv + h)),
            pl.BlockSpec((r, _DV), lambda b, h, i: (row(b, h, i), 2 * kv + _HEADS + h)),
            pl.BlockSpec((r, _LANES), lambda b, h, i: (row(b, h, i), 0)),
            pl.BlockSpec((_LANES, _DK), lambda b, h, i: (0, h)),
            pl.BlockSpec((1, _DK), lambda b, h, i: (0, h)),
            pl.BlockSpec((1, _DV), lambda b, h, i: (0, 0)),
            pl.BlockSpec((1, _DK, _DV), lambda b, h, i: (h, 0, 0)),
        ],
        out_specs=[
            pl.BlockSpec((r, _DV), lambda b, h, i: (row(b, h, i), h)),
            pl.BlockSpec((1, 1, _DK, _DV), lambda b, h, i: (b, h, 0, 0)),
        ],
        out_shape=[
            jax.ShapeDtypeStruct((m, _V), _BF16),
            jax.ShapeDtypeStruct((n_batch, _HEADS, _DK, _DV), _F32),
        ],
        scratch_shapes=[pltpu.VMEM((_DK, _DV), _F32), pltpu.VMEM((chunk, _DK), _F32), pltpu.VMEM((chunk, _DK), _F32),
                        pltpu.VMEM((chunk, _DV), _F32)],
        compiler_params=_params(blocks, scratch_bytes=4 * _DK * _DV * 4, n_grid=3),
        name="gla",
    )(proj, proj, proj, proj, a_low, wa2, ba.reshape(1, _QK), hnorm.reshape(1, _DV), s0)
    return outs[0], outs[1]


def _conv_in_kernel(a_ref, wb_ref, wc_ref, wu_ref, s_ref, am_ref, sm_ref, cw_ref, o_ref, tail_scr, meta_scr,
                    *, blocks_per_seq, row_split):
    i, j = pl.program_id(0), pl.program_id(1)

    @pl.when(i == 0)
    def _meta_rows():
        am, sm = am_ref[...], sm_ref[...]
        z_meta = ((jnp.dot(am, wc_ref[0], preferred_element_type=_F32) * sm)
                  * (jnp.dot(am, wu_ref[0], preferred_element_type=_F32) * sm))
        meta_scr[j] = z_meta[_N_META - _SUBLANES:]
        tail_scr[j] = z_meta[_N_META - _SUBLANES:]

    prev = jnp.where(i % blocks_per_seq == 0, meta_scr[j], tail_scr[j])
    cw = cw_ref[...]
    sub_rows = o_ref.shape[0] // row_split
    rid = lax.broadcasted_iota(jnp.int32, (sub_rows, o_ref.shape[1]), 0)
    for h in range(row_split):
        rows = slice(h * sub_rows, (h + 1) * sub_rows)
        a, s = a_ref[rows, :], s_ref[rows, :]
        bg = jnp.dot(a, wb_ref[0], preferred_element_type=_F32) * s
        z = ((jnp.dot(a, wc_ref[0], preferred_element_type=_F32) * s)
             * (jnp.dot(a, wu_ref[0], preferred_element_type=_F32) * s))
        z_m1, z_m2 = prev[_SUBLANES - 1:_SUBLANES], prev[_SUBLANES - 2:_SUBLANES - 1]
        z1 = jnp.where(rid == 0, z_m1, pltpu.roll(z, 1, 0))
        z2 = jnp.where(rid == 0, z_m2, jnp.where(rid == 1, z_m1, pltpu.roll(z, 2, 0)))
        conv = cw[2:3] * z + cw[0:1] * z2 + cw[1:2] * z1
        o_ref[rows, :] = (bg * conv).astype(o_ref.dtype)
        prev = z[sub_rows - _SUBLANES:]
    tail_scr[j] = prev


def _conv_in(a, rstd, a_meta, rstd_meta, w, conv_w, *, seq, bm, bn, row_split):
    m, k = a.shape
    assert m % bm == 0 and seq % bm == 0 and _D % bn == 0
    n_i, n_j = m // bm, _D // bn
    kernel = functools.partial(_conv_in_kernel, blocks_per_seq=seq // bm, row_split=row_split)
    blocks = [bm * k * 2, 3 * k * bn * 2, bm * _LANES * 4, _N_META * k * 2, _CONV_WIDTH * bn * 4, bm * bn * 2]
    return pl.pallas_call(
        kernel,
        grid=(n_i, n_j),
        in_specs=[
            pl.BlockSpec((bm, k), lambda i, j: (i, 0)),
            pl.BlockSpec((1, k, bn), lambda i, j: (j, 0, 0)),
            pl.BlockSpec((1, k, bn), lambda i, j: (n_j + j, 0, 0)),
            pl.BlockSpec((1, k, bn), lambda i, j: (2 * n_j + j, 0, 0)),
            pl.BlockSpec((bm, 1), lambda i, j: (i, 0)),
            pl.BlockSpec((_N_META, k), lambda i, j: (0, 0)),
            pl.BlockSpec((_N_META, 1), lambda i, j: (0, 0)),
            pl.BlockSpec((_CONV_WIDTH, bn), lambda i, j: (0, j)),
        ],
        out_specs=pl.BlockSpec((bm, bn), lambda i, j: (i, j)),
        out_shape=jax.ShapeDtypeStruct((m, _D), _BF16),
        scratch_shapes=[pltpu.VMEM((n_j, _SUBLANES, bn), _F32), pltpu.VMEM((n_j, _SUBLANES, bn), _F32)],
        compiler_params=_params(blocks, scratch_bytes=8 * (bm // row_split) * bn * 4),
        name="conv_in",
    )(a, w, w, w, rstd, a_meta, rstd_meta, conv_w)


def kernel(x, meta, norm_mix, norm_ffn, gla_w_in, gla_w_a2, gla_b_a, gla_head_norm, gla_w_out,
           conv_w_in, conv_w, conv_w_out, ffn_w_gate, ffn_w_up, ffn_w_down, norm_final):
    n_batch, seq, d = x.shape
    xm = x.reshape(n_batch * seq, d)
    xmeta = meta.astype(x.dtype)
    n_qkvr = 2 * _QK + 2 * _V
    d_ff = ffn_w_gate.shape[2]
    assert d_ff % 256 == 0


    w_in_t = jnp.transpose(gla_w_in, (0, 2, 1))
    w_qkvr = _cast_rows(w_in_t, 0, norm_mix[0], 0, n_qkvr, tile=1024, rb=512)
    w_a1 = _cast_rows(w_in_t, 0, norm_mix[0], n_qkvr, _RANK, tile=_LANES, rb=_LANES)
    w_a2 = jnp.pad(gla_w_a2[0], ((0, _LANES - _RANK), (0, 0))).astype(_BF16)

    hb, rstd = _row_stats(xm)
    hb_meta, rstd_meta = _row_stats(xmeta)
    proj, proj_meta, (w_out0, w_gate0, w_up0) = _matmul(
        hb, [(w_qkvr, 0)], n_qkvr, _epilogue_cast, _BF16, bm=1024, bn=1024, scale=(rstd, rstd_meta),
        meta=(hb_meta, None),
        jobs=[_CastJob(gla_w_out, 0, 64, 512), _CastJob(ffn_w_gate, 0, 128, 256, norm_ffn[0]),
              _CastJob(ffn_w_up, 0, 128, 256, norm_ffn[0])],
        row_split=4, name="gla_in")
    a_low, a_low_meta, _ = _matmul(hb, [(w_a1, 0)], _LANES, _epilogue_cast, _F32, bm=1024, bn=_LANES,
                                   scale=(rstd, rstd_meta), meta=(hb_meta, None), name="gla_gate_in")

    zero_state = jnp.zeros((_HEADS, _DK, _DV), _F32)
    o_meta, s_meta = _gla(proj_meta, a_low_meta, w_a2, gla_b_a[0], gla_head_norm[0], zero_state,
                          n_batch=1, seq=_N_META, chunk=_N_META, rows_per_step=_N_META)
    o_x, _ = _gla(proj, a_low, w_a2, gla_b_a[0], gla_head_norm[0], s_meta[0],
                  n_batch=n_batch, seq=seq, chunk=64, rows_per_step=1024)
    h, hb, rstd, h_meta, (w_down0,) = _matmul(
        o_x, [(w_out0, 0)], d, _epilogue_resid, _F32, bm=1024, bn=512, resid=xm, meta=(o_meta, xmeta),
        emit_stats=True, jobs=[_CastJob(ffn_w_down, 0, d_ff // 128, 512)], row_split=4, name="gla_out")

    hb_meta, rstd_meta = _row_stats(h_meta)
    act, act_meta, (w_cout, w_gate1, w_up1, w_cin) = _matmul(
        hb, [(w_gate0, 0), (w_up0, 0)], d_ff, _epilogue_swiglu, _BF16, bm=2048, bn=256, scale=(rstd, rstd_meta),
        meta=(hb_meta, None),
        jobs=[_CastJob(conv_w_out, 0, 64, 512), _CastJob(ffn_w_gate, 1, 256, 256, norm_ffn[1]),
              _CastJob(ffn_w_up, 1, 256, 256, norm_ffn[1]), _CastJob(conv_w_in, 0, 256, 256, norm_mix[1])],
        row_split=8, name="ffn_gate_up")
    h, hb, rstd, h_meta, (w_down1,) = _matmul(
        act, [(w_down0, 0)], d, _epilogue_resid, _F32, bm=512, bn=512, resid=h, meta=(act_meta, h_meta),
        emit_stats=True, jobs=[_CastJob(ffn_w_down, 1, d_ff // 64, 512)], row_split=2, name="ffn_down")

    hb_meta, rstd_meta = _row_stats(h_meta)
    y = _conv_in(hb, rstd, hb_meta, rstd_meta, w_cin, conv_w[0], seq=seq, bm=2048, bn=256, row_split=8)
    h, hb, rstd, _, _ = _matmul(y, [(w_cout, 0)], d, _epilogue_resid, _F32, bm=1024, bn=512, resid=h,
                                emit_stats=True, row_split=4, name="conv_out")

    act, _, _ = _matmul(hb, [(w_gate1, 0), (w_up1, 0)], d_ff, _epilogue_swiglu, _BF16, bm=2048, bn=256,
                        scale=(rstd, None), row_split=8, name="ffn_gate_up")
    h, _, _ = _matmul(act, [(w_down1, 0)], d, _epilogue_resid, _F32, bm=512, bn=512, resid=h, row_split=2,
                      name="ffn_down")
    out = _rmsnorm(h, norm_final, _F32)
    return out.reshape(n_batch, seq, d)
```

```python
import functools
from typing import NamedTuple

import jax
import jax.numpy as jnp
from jax import lax
from jax.experimental import pallas as pl
from jax.experimental.pallas import tpu as pltpu

_D = 4096
_N_META = 16
_HEADS = 4
_DK = 512
_DV = 1024
_QK = _HEADS * _DK
_V = _HEADS * _DV
_RANK = 16
_GATE_TAU = 16.0
_CONV_WIDTH = 3
_EPS = 1e-6
_LOG2E = 1.4426950408889634

_LANES = 128
_SUBLANES = 8
_BF16_SUBLANES = 16
_V7X_SCOPED_VMEM_BYTES = 60000 * 1024

_F32 = jnp.float32
_BF16 = jnp.bfloat16


def _params(block_bytes, scratch_bytes=0, n_grid=2):
    need = 2 * sum(block_bytes) + scratch_bytes
    assert need <= _V7X_SCOPED_VMEM_BYTES, (need, _V7X_SCOPED_VMEM_BYTES)
    return pltpu.CompilerParams(
        dimension_semantics=("arbitrary",) * n_grid,
        vmem_limit_bytes=_V7X_SCOPED_VMEM_BYTES,
    )


class _CastJob(NamedTuple):
    src: jax.Array
    layer: int
    n_blocks: int
    tile_cols: int
    gain: jax.Array = None

    @property
    def block(self):
        rows, cols = self.src.shape[1:]
        assert rows % self.n_blocks == 0 and (rows // self.n_blocks) % _BF16_SUBLANES == 0
        assert cols % self.tile_cols == 0
        return rows // self.n_blocks, cols


def _job_plumbing(jobs, n_steps, flat_step):
    in_specs, args, out_specs, out_shapes, blocks, desc = [], [], [], [], [], []
    for job in jobs:
        assert job.n_blocks <= n_steps, (job.n_blocks, n_steps)
        rb, cols = job.block
        last, layer = job.n_blocks - 1, job.layer

        def blk(*g, last=last):
            return jnp.minimum(flat_step(*g), last)

        in_specs.append(pl.BlockSpec((1, rb, cols), lambda *g, blk=blk, layer=layer: (layer, blk(*g), 0)))
        args.append(job.src)
        if job.gain is not None:
            in_specs.append(pl.BlockSpec((rb, 1), lambda *g, blk=blk: (blk(*g), 0)))
            args.append(job.gain.reshape(-1, 1))
        n_tiles, tc = cols // job.tile_cols, job.tile_cols
        out_specs.append(pl.BlockSpec((n_tiles, rb, tc), lambda *g, blk=blk: (0, blk(*g), 0)))
        out_shapes.append(jax.ShapeDtypeStruct((n_tiles, job.src.shape[1], tc), _BF16))
        blocks += [rb * cols * 4, rb * cols * 2, rb * _LANES * 4]
        desc.append(job.gain is not None)
    return in_specs, args, out_specs, out_shapes, blocks, tuple(desc)


def _take_job_inputs(it, desc):
    return [(next(it), next(it) if has_gain else None) for has_gain in desc]


def _run_casts(in_refs, dst_refs):
    for (src, gain), dst in zip(in_refs, dst_refs):
        w = src[0]
        if gain is not None:
            w = w * gain[...]
        w = w.astype(_BF16)
        n_tiles, _, tc = dst.shape
        for t in range(n_tiles):
            dst[t] = w[:, t * tc:(t + 1) * tc]


def _cast_rows_kernel(src_ref, gain_ref, dst_ref, *, n_valid):
    w = src_ref[0] * gain_ref[...]
    if n_valid < w.shape[0]:
        w = jnp.where(lax.broadcasted_iota(jnp.int32, w.shape, 0) < n_valid, w, 0.0)
    dst_ref[0] = jnp.transpose(w).astype(_BF16)


def _cast_rows(src_t, layer, gain, row0, n_rows, tile, rb):
    src_rows, cols = src_t.shape[1:]
    n_valid = min(n_rows, src_rows - row0)
    n_tiles = -(-n_rows // tile)
    per_tile = tile // rb
    assert tile % rb == 0 and row0 % rb == 0 and (n_valid == n_tiles * tile or (n_tiles == 1 and rb == tile))
    r0 = row0 // rb
    return pl.pallas_call(
        functools.partial(_cast_rows_kernel, n_valid=n_valid),
        grid=(n_tiles * per_tile,),
        in_specs=[pl.BlockSpec((1, rb, cols), lambda i: (layer, r0 + i, 0)),
                  pl.BlockSpec((1, cols), lambda i: (0, 0))],
        out_specs=pl.BlockSpec((1, cols, rb), lambda i: (i // per_tile, 0, i % per_tile)),
        out_shape=jax.ShapeDtypeStruct((n_tiles, cols, tile), _BF16),
        compiler_params=_params([rb * cols * 4, rb * cols * 2, cols * 4], scratch_bytes=2 * rb * cols * 4, n_grid=1),
        name="weight_cast",
    )(src_t, gain.reshape(1, -1))


def _row_stats_kernel(x_ref, xb_ref, rstd_ref):
    x = x_ref[...]
    xb_ref[...] = x.astype(_BF16)
    rstd_ref[...] = lax.rsqrt(jnp.mean(x * x, axis=-1, keepdims=True) + _EPS)


def _row_stats(x, bm=256):
    m, d = x.shape
    bm = min(bm, m)
    assert m % bm == 0
    return pl.pallas_call(
        _row_stats_kernel,
        grid=(m // bm,),
        in_specs=[pl.BlockSpec((bm, d), lambda i: (i, 0))],
        out_specs=[pl.BlockSpec((bm, d), lambda i: (i, 0)), pl.BlockSpec((bm, 1), lambda i: (i, 0))],
        out_shape=[jax.ShapeDtypeStruct((m, d), _BF16), jax.ShapeDtypeStruct((m, 1), _F32)],
        compiler_params=_params([bm * d * 4, bm * d * 2, bm * _LANES * 4], scratch_bytes=2 * bm * d * 4, n_grid=1),
        name="row_stats",
    )(x)


def _rmsnorm_kernel(x_ref, g_ref, o_ref):
    x = x_ref[...]
    ms = jnp.mean(x * x, axis=-1, keepdims=True)
    o_ref[...] = (x * lax.rsqrt(ms + _EPS) * g_ref[...]).astype(o_ref.dtype)


def _rmsnorm(x, g, out_dtype, bm=256):
    m, d = x.shape
    bm = min(bm, m)
    assert m % bm == 0
    blocks = [bm * d * 4, d * 4, bm * d * jnp.dtype(out_dtype).itemsize]
    return pl.pallas_call(
        _rmsnorm_kernel,
        grid=(m // bm,),
        in_specs=[pl.BlockSpec((bm, d), lambda i: (i, 0)), pl.BlockSpec((1, d), lambda i: (0, 0))],
        out_specs=pl.BlockSpec((bm, d), lambda i: (i, 0)),
        out_shape=jax.ShapeDtypeStruct((m, d), out_dtype),
        compiler_params=_params(blocks, scratch_bytes=2 * bm * d * 4, n_grid=1),
        name="rmsnorm",
    )(x, g.reshape(1, d))


def _epilogue_cast(accs, resid):
    return accs[0]


def _epilogue_resid(accs, resid):
    return resid + accs[0]


def _epilogue_swiglu(accs, resid):
    gate, up = accs
    return gate * jax.nn.sigmoid(gate) * up


def _matmul_kernel(*refs, w_transposed, epilogue, has_resid, has_scale, has_meta, emit_stats, jobs, n_cols,
                   row_split):
    it = iter(refs)
    a_ref = next(it)
    w_refs = [next(it) for _ in w_transposed]
    r_ref = next(it) if has_resid else None
    s_ref = next(it) if has_scale else None
    am_ref = next(it) if has_meta else None
    rm_ref = next(it) if has_meta and has_resid else None
    sm_ref = next(it) if has_meta and has_scale else None
    job_in = _take_job_inputs(it, jobs)
    o_ref = next(it)
    ob_ref, rstd_ref = (next(it), next(it)) if emit_stats else (None, None)
    om_ref = next(it) if has_meta else None
    dst_refs = [next(it) for _ in jobs]
    ssq_scr = next(it) if emit_stats else None
    i, j = pl.program_id(0), pl.program_id(1)

    def result(a, resid, scale):
        accs = [lax.dot_general(a, w[0], (((1,), (1 if t else 0,)), ((), ())), preferred_element_type=_F32)
                for w, t in zip(w_refs, w_transposed)]
        if scale is not None:
            accs = [acc * scale for acc in accs]
        return epilogue(accs, resid)

    if emit_stats:
        @pl.when(j == 0)
        def _reset():
            ssq_scr[...] = jnp.zeros(ssq_scr.shape, _F32)

    sub_rows = o_ref.shape[0] // row_split
    for h in range(row_split):
        rows = slice(h * sub_rows, (h + 1) * sub_rows)
        out = result(a_ref[rows, :], r_ref[rows, :] if has_resid else None, s_ref[rows, :] if has_scale else None)
        o_ref[rows, :] = out.astype(o_ref.dtype)
        if emit_stats:
            ob_ref[rows, :] = out.astype(_BF16)
            ssq = ssq_scr[rows, :] + jnp.sum(out * out, axis=-1, keepdims=True)
            ssq_scr[rows, :] = ssq
            rstd_ref[rows, :] = lax.rsqrt(ssq * (1.0 / n_cols) + _EPS)
        _run_casts(job_in[h::row_split], dst_refs[h::row_split])

    if has_meta:
        @pl.when(i == 0)
        def _meta_rows():
            out_m = result(am_ref[...], rm_ref[...] if has_resid else None, sm_ref[...] if has_scale else None)
            om_ref[0] = out_m.astype(om_ref.dtype)

        @pl.when(i != 0)
        def _meta_unused():
            om_ref[0] = jnp.zeros(om_ref.shape[1:], om_ref.dtype)


def _matmul(a, ws, n_out, epilogue, out_dtype, bm, bn, resid=None, scale=None, meta=None, emit_stats=False,
            jobs=(), row_split=1, name="matmul"):
    m, k = a.shape
    bm = min(bm, m)
    assert m % bm == 0 and n_out % bn == 0
    n_i, n_j = m // bm, n_out // bn
    out_bytes = jnp.dtype(out_dtype).itemsize
    has_resid, has_scale, has_meta = resid is not None, scale is not None, meta is not None

    in_specs = [pl.BlockSpec((bm, k), lambda i, j: (i, 0))]
    args = [a]
    blocks = [bm * k * 2, bm * bn * out_bytes]
    w_transposed = tuple(len(entry) > 2 and entry[2] for entry in ws)
    for (w, off, *_), transposed in zip(ws, w_transposed):
        tile = (bn, k) if transposed else (k, bn)
        assert w.shape[1:] == tile, (w.shape, tile)
        in_specs.append(pl.BlockSpec((1, *tile), lambda i, j, off=off: (off + j, 0, 0)))
        args.append(w)
        blocks.append(k * bn * 2)
    if has_resid:
        in_specs.append(pl.BlockSpec((bm, bn), lambda i, j: (i, j)))
        args.append(resid)
        blocks.append(bm * bn * 4)
    if has_scale:
        in_specs.append(pl.BlockSpec((bm, 1), lambda i, j: (i, 0)))
        args.append(scale[0])
        blocks.append(bm * _LANES * 4)
    out_specs = [pl.BlockSpec((bm, bn), lambda i, j: (i, j))]
    out_shapes = [jax.ShapeDtypeStruct((m, n_out), out_dtype)]
    scratch = []
    if emit_stats:
        out_specs += [pl.BlockSpec((bm, bn), lambda i, j: (i, j)), pl.BlockSpec((bm, 1), lambda i, j: (i, 0))]
        out_shapes += [jax.ShapeDtypeStruct((m, n_out), _BF16), jax.ShapeDtypeStruct((m, 1), _F32)]
        blocks += [bm * bn * 2, bm * _LANES * 4]
        scratch.append(pltpu.VMEM((bm, 1), _F32))
    if has_meta:
        a_meta, r_meta = meta
        in_specs.append(pl.BlockSpec((_N_META, k), lambda i, j: (0, 0)))
        args.append(a_meta)
        blocks.append(_N_META * k * 2)
        if has_resid:
            in_specs.append(pl.BlockSpec((_N_META, bn), lambda i, j: (0, j)))
            args.append(r_meta)
        if has_scale:
            in_specs.append(pl.BlockSpec((_N_META, 1), lambda i, j: (0, 0)))
            args.append(scale[1])
        out_specs.append(pl.BlockSpec((1, _N_META, bn), lambda i, j: (i, 0, j)))
        out_shapes.append(jax.ShapeDtypeStruct((n_i, _N_META, n_out), out_dtype))
    j_in, j_args, j_out, j_shapes, j_blocks, j_desc = _job_plumbing(jobs, n_i * n_j, lambda i, j: i * n_j + j)

    kernel = functools.partial(_matmul_kernel, w_transposed=w_transposed, epilogue=epilogue, has_resid=has_resid,
                               has_scale=has_scale, has_meta=has_meta, emit_stats=emit_stats, jobs=j_desc,
                               n_cols=n_out, row_split=row_split)
    outs = pl.pallas_call(
        kernel,
        grid=(n_i, n_j),
        in_specs=in_specs + j_in,
        out_specs=out_specs + j_out,
        out_shape=out_shapes + j_shapes,
        scratch_shapes=scratch,
        compiler_params=_params(blocks + j_blocks,
                                scratch_bytes=(len(ws) + 1 + emit_stats) * (bm // row_split) * bn * 4),
        name=name,
    )(*args, *j_args)
    n_main = 3 if emit_stats else 1
    main = list(outs[:n_main])
    out_meta = outs[n_main][0] if has_meta else None
    return (*main, out_meta, list(outs[n_main + has_meta:]))


def _cumsum_rows(tri, g):
    g1 = g.astype(_BF16)
    r1 = g - g1.astype(_F32)
    g2 = r1.astype(_BF16)
    g3 = (r1 - g2.astype(_F32)).astype(_BF16)
    out = jnp.dot(tri, g1, preferred_element_type=_F32)
    out = out + jnp.dot(tri, g2, preferred_element_type=_F32)
    return out + jnp.dot(tri, g3, preferred_element_type=_F32)


def _gla_kernel(q_ref, k_ref, v_ref, r_ref, al_ref, wa2_ref, ba_ref, hnorm_ref, s0_ref, o_ref, sfin_ref,
                s_scr, b_scr, bnext_scr, o_scr, *, chunk, n_chunks, sub):
    step = pl.program_id(2)

    @pl.when(step == 0)
    def _load_state():
        s_scr[...] = s0_ref[0]

    c = chunk
    state_tile = 2 * _LANES
    n_sub = c // sub
    tri = (lax.broadcasted_iota(jnp.int32, (c, c), 0) >= lax.broadcasted_iota(jnp.int32, (c, c), 1)).astype(_BF16)
    sub_row = lax.broadcasted_iota(jnp.int32, (sub, c), 0)
    sub_col = lax.broadcasted_iota(jnp.int32, (sub, c), 1)

    q_scale = _DK ** -0.5
    trans_b = (((1,), (1,)), ((), ()))

    def gate_of_chunk(ci):
        rows = pl.ds(ci * c if isinstance(ci, int) else pl.multiple_of(ci * c, c), c)
        return jnp.dot(al_ref[rows, :].astype(_BF16), wa2_ref[...], preferred_element_type=_F32) + ba_ref[...]

    def decay_of_gate(x):
        g = (jnp.minimum(x, 0.0) - jnp.log(1.0 + jnp.exp(-jnp.abs(x)))) * (_LOG2E / _GATE_TAU)
        return _cumsum_rows(tri, g)

    def one_chunk(ci, carry):
        base = pl.multiple_of(ci * c, c)
        rows = pl.ds(base, c)
        b_scr[...] = bnext_scr[...]
        b_last = b_scr[c - 1:c, :]
        x_next = gate_of_chunk(jnp.minimum(ci + 1, n_chunks - 1))

        q_dec = (q_ref[rows, :].astype(_F32) * q_scale * jnp.exp2(b_scr[...])).astype(_BF16)
        o_scr[...] = jnp.dot(q_dec, s_scr[...].astype(_BF16), preferred_element_type=_F32)

        diag, off = [], [None]
        for blk in range(n_sub):
            r0 = blk * sub
            b_b = b_scr[r0:r0 + sub, :]
            q_b = q_ref[pl.ds(base + r0, sub), :].astype(_F32) * q_scale
            stacked = jnp.concatenate(
                [q_b * jnp.exp2(b_b - b_scr[r0 + j:r0 + j + 1, :]) for j in range(sub)], axis=0).astype(_BF16)
            diag.append(lax.dot_general(stacked, k_ref[rows, :], trans_b, preferred_element_type=_F32))
            if blk > 0:
                p = b_scr[r0 - 1:r0, :]
                q_o = (q_b * jnp.exp2(b_b - p)).astype(_BF16)
                n_keys = -(-r0 // _BF16_SUBLANES) * _BF16_SUBLANES
                k_o = (k_ref[pl.ds(base, n_keys), :].astype(_F32) * jnp.exp2(p - b_scr[0:n_keys, :])).astype(_BF16)
                if n_keys < c:
                    k_o = jnp.concatenate([k_o, jnp.zeros((c - n_keys, _DK), _BF16)], axis=0)
                off.append(lax.dot_general(q_o, k_o, trans_b, preferred_element_type=_F32))

        b_next = decay_of_gate(x_next)

        tiles = []
        for blk in range(n_sub):
            r0 = blk * sub
            sc = jnp.zeros((sub, c), _F32)
            for j in range(sub):
                sc = jnp.where(sub_col == r0 + j, diag[blk][j * sub:(j + 1) * sub], sc)
            sc = jnp.where(sub_row + r0 >= sub_col, sc, 0.0)
            if blk > 0:
                sc = sc + jnp.where(sub_col < r0, off[blk], 0.0)
            tiles.append(sc)
        scores = jnp.concatenate(tiles, axis=0) if n_sub > 1 else tiles[0]
        o_intra = jnp.dot(scores.astype(_BF16), v_ref[rows, :], preferred_element_type=_F32)
        bnext_scr[...] = b_next

        k_dec = (k_ref[rows, :].astype(_F32) * jnp.exp2(b_last - b_scr[...])).astype(_BF16)
        decay_col = jnp.transpose(jnp.broadcast_to(jnp.exp2(b_last), (_LANES, _DK)))
        for t in range(_DV // state_tile):
            cols = slice(t * state_tile, (t + 1) * state_tile)
            upd = lax.dot_general(k_dec, v_ref[rows, cols], (((0,), (0,)), ((), ())), preferred_element_type=_F32)
            s_scr[:, cols] = s_scr[:, cols] * jnp.tile(decay_col, (1, state_tile // _LANES)) + upd

        o = o_scr[...] + o_intra
        ms = jnp.mean(o * o, axis=-1, keepdims=True)
        o_n = o * lax.rsqrt(ms + _EPS) * hnorm_ref[...]
        r = r_ref[rows, :].astype(_F32)
        o_ref[rows, :] = (o_n * (r * jax.nn.sigmoid(r))).astype(o_ref.dtype)
        return carry

    bnext_scr[...] = decay_of_gate(gate_of_chunk(0))
    lax.fori_loop(0, n_chunks, one_chunk, 0)

    @pl.when(step == pl.num_programs(2) - 1)
    def _store_state():
        sfin_ref[0, 0] = s_scr[...]


def _gla(proj, a_low, wa2, ba, hnorm, s0, *, n_batch, seq, chunk, rows_per_step):
    m = proj.shape[0]
    assert m == n_batch * seq and seq % rows_per_step == 0 and rows_per_step % chunk == 0
    steps = seq // rows_per_step
    r = rows_per_step
    kv = _QK // _DV

    def row(b, h, i):
        return b * steps + i

    kernel = functools.partial(_gla_kernel, chunk=chunk, n_chunks=r // chunk, sub=min(chunk, 8))
    blocks = [r * _DK * 2, r * _DK * 2, r * _DV * 2, r * _DV * 2, r * _LANES * 4, _LANES * _DK * 2,
              _DK * 4, _DV * 4, _DK * _DV * 4, r * _DV * 2, _DK * _DV * 4]
    outs = pl.pallas_call(
        kernel,
        grid=(n_batch, _HEADS, steps),
        in_specs=[
            pl.BlockSpec((r, _DK), lambda b, h, i: (row(b, h, i), h)),
            pl.BlockSpec((r, _DK), lambda b, h, i: (row(b, h, i), _HEADS + h)),
            pl.BlockSpec((r, _DV), lambda b, h, i: (row(b, h, i), 2 * kv + h)),
            pl.BlockSpec((r, _DV), lambda b, h, i: (row(b, h, i), 2 * kv + _HEADS + h)),
            pl.BlockSpec((r, _LANES), lambda b, h, i: (row(b, h, i), 0)),
            pl.BlockSpec((_LANES, _DK), lambda b, h, i: (0, h)),
            pl.BlockSpec((1, _DK), lambda b, h, i: (0, h)),
            pl.BlockSpec((1, _DV), lambda b, h, i: (0, 0)),
            pl.BlockSpec((1, _DK, _DV), lambda b, h, i: (h, 0, 0)),
        ],
        out_specs=[
            pl.BlockSpec((r, _DV), lambda b, h, i: (row(b, h, i), h)),
            pl.BlockSpec((1, 1, _DK, _DV), lambda b, h, i: (b, h, 0, 0)),
        ],
        out_shape=[
            jax.ShapeDtypeStruct((m, _V), _BF16),
            jax.ShapeDtypeStruct((n_batch, _HEADS, _DK, _DV), _F32),
        ],
        scratch_shapes=[pltpu.VMEM((_DK, _DV), _F32), pltpu.VMEM((chunk, _DK), _F32), pltpu.VMEM((chunk, _DK), _F32),
                        pltpu.VMEM((chunk, _DV), _F32)],
        compiler_params=_params(blocks, scratch_bytes=4 * _DK * _DV * 4, n_grid=3),
        name="gla",
    )(proj, proj, proj, proj, a_low, wa2, ba.reshape(1, _QK), hnorm.reshape(1, _DV), s0)
    return outs[0], outs[1]


def _conv_in_kernel(a_ref, wb_ref, wc_ref, wu_ref, s_ref, am_ref, sm_ref, cw_ref, o_ref, tail_scr, meta_scr,
                    *, blocks_per_seq, row_split):
    i, j = pl.program_id(0), pl.program_id(1)

    @pl.when(i == 0)
    def _meta_rows():
        am, sm = am_ref[...], sm_ref[...]
        z_meta = ((jnp.dot(am, wc_ref[0], preferred_element_type=_F32) * sm)
                  * (jnp.dot(am, wu_ref[0], preferred_element_type=_F32) * sm))
        meta_scr[j] = z_meta[_N_META - _SUBLANES:]
        tail_scr[j] = z_meta[_N_META - _SUBLANES:]

    prev = jnp.where(i % blocks_per_seq == 0, meta_scr[j], tail_scr[j])
    cw = cw_ref[...]
    sub_rows = o_ref.shape[0] // row_split
    rid = lax.broadcasted_iota(jnp.int32, (sub_rows, o_ref.shape[1]), 0)
    for h in range(row_split):
        rows = slice(h * sub_rows, (h + 1) * sub_rows)
        a, s = a_ref[rows, :], s_ref[rows, :]
        bg = jnp.dot(a, wb_ref[0], preferred_element_type=_F32) * s
        z = ((jnp.dot(a, wc_ref[0], preferred_element_type=_F32) * s)
             * (jnp.dot(a, wu_ref[0], preferred_element_type=_F32) * s))
        z_m1, z_m2 = prev[_SUBLANES - 1:_SUBLANES], prev[_SUBLANES - 2:_SUBLANES - 1]
        z1 = jnp.where(rid == 0, z_m1, pltpu.roll(z, 1, 0))
        z2 = jnp.where(rid == 0, z_m2, jnp.where(rid == 1, z_m1, pltpu.roll(z, 2, 0)))
        conv = cw[2:3] * z + cw[0:1] * z2 + cw[1:2] * z1
        o_ref[rows, :] = (bg * conv).astype(o_ref.dtype)
        prev = z[sub_rows - _SUBLANES:]
    tail_scr[j] = prev


def _conv_in(a, rstd, a_meta, rstd_meta, w, conv_w, *, seq, bm, bn, row_split):
    m, k = a.shape
    assert m % bm == 0 and seq % bm == 0 and _D % bn == 0
    n_i, n_j = m // bm, _D // bn
    kernel = functools.partial(_conv_in_kernel, blocks_per_seq=seq // bm, row_split=row_split)
    blocks = [bm * k * 2, 3 * k * bn * 2, bm * _LANES * 4, _N_META * k * 2, _CONV_WIDTH * bn * 4, bm * bn * 2]
    return pl.pallas_call(
        kernel,
        grid=(n_i, n_j),
        in_specs=[
            pl.BlockSpec((bm, k), lambda i, j: (i, 0)),
            pl.BlockSpec((1, k, bn), lambda i, j: (j, 0, 0)),
            pl.BlockSpec((1, k, bn), lambda i, j: (n_j + j, 0, 0)),
            pl.BlockSpec((1, k, bn), lambda i, j: (2 * n_j + j, 0, 0)),
            pl.BlockSpec((bm, 1), lambda i, j: (i, 0)),
            pl.BlockSpec((_N_META, k), lambda i, j: (0, 0)),
            pl.BlockSpec((_N_META, 1), lambda i, j: (0, 0)),
            pl.BlockSpec((_CONV_WIDTH, bn), lambda i, j: (0, j)),
        ],
        out_specs=pl.BlockSpec((bm, bn), lambda i, j: (i, j)),
        out_shape=jax.ShapeDtypeStruct((m, _D), _BF16),
        scratch_shapes=[pltpu.VMEM((n_j, _SUBLANES, bn), _F32), pltpu.VMEM((n_j, _SUBLANES, bn), _F32)],
        compiler_params=_params(blocks, scratch_bytes=8 * (bm // row_split) * bn * 4),
        name="conv_in",
    )(a, w, w, w, rstd, a_meta, rstd_meta, conv_w)


def kernel(x, meta, norm_mix, norm_ffn, gla_w_in, gla_w_a2, gla_b_a, gla_head_norm, gla_w_out,
           conv_w_in, conv_w, conv_w_out, ffn_w_gate, ffn_w_up, ffn_w_down, norm_final):
    n_batch, seq, d = x.shape
    xm = x.reshape(n_batch * seq, d)
    xmeta = meta.astype(x.dtype)
    n_qkvr = 2 * _QK + 2 * _V
    d_ff = ffn_w_gate.shape[2]
    assert d_ff % 256 == 0


    w_in_t = jnp.transpose(gla_w_in, (0, 2, 1))
    w_qkvr = _cast_rows(w_in_t, 0, norm_mix[0], 0, n_qkvr, tile=1024, rb=512)
    w_a1 = _cast_rows(w_in_t, 0, norm_mix[0], n_qkvr, _RANK, tile=_LANES, rb=_LANES)
    w_a2 = jnp.pad(gla_w_a2[0], ((0, _LANES - _RANK), (0, 0))).astype(_BF16)

    hb, rstd = _row_stats(xm)
    hb_meta, rstd_meta = _row_stats(xmeta)
    proj, proj_meta, (w_out0, w_gate0, w_up0) = _matmul(
        hb, [(w_qkvr, 0)], n_qkvr, _epilogue_cast, _BF16, bm=1024, bn=1024, scale=(rstd, rstd_meta),
        meta=(hb_meta, None),
        jobs=[_CastJob(gla_w_out, 0, 64, 512), _CastJob(ffn_w_gate, 0, 128, 256, norm_ffn[0]),
              _CastJob(ffn_w_up, 0, 128, 256, norm_ffn[0])],
        row_split=4, name="gla_in")
    a_low, a_low_meta, _ = _matmul(hb, [(w_a1, 0)], _LANES, _epilogue_cast, _F32, bm=1024, bn=_LANES,
                                   scale=(rstd, rstd_meta), meta=(hb_meta, None), name="gla_gate_in")

    zero_state = jnp.zeros((_HEADS, _DK, _DV), _F32)
    o_meta, s_meta = _gla(proj_meta, a_low_meta, w_a2, gla_b_a[0], gla_head_norm[0], zero_state,
                          n_batch=1, seq=_N_META, chunk=_N_META, rows_per_step=_N_META)
    o_x, _ = _gla(proj, a_low, w_a2, gla_b_a[0], gla_head_norm[0], s_meta[0],
                  n_batch=n_batch, seq=seq, chunk=128, rows_per_step=1024)
    h, hb, rstd, h_meta, (w_down0,) = _matmul(
        o_x, [(w_out0, 0)], d, _epilogue_resid, _F32, bm=1024, bn=512, resid=xm, meta=(o_meta, xmeta),
        emit_stats=True, jobs=[_CastJob(ffn_w_down, 0, d_ff // 128, 512)], row_split=4, name="gla_out")

    hb_meta, rstd_meta = _row_stats(h_meta)
    act, act_meta, (w_cout, w_gate1, w_up1, w_cin) = _matmul(
        hb, [(w_gate0, 0), (w_up0, 0)], d_ff, _epilogue_swiglu, _BF16, bm=2048, bn=256, scale=(rstd, rstd_meta),
        meta=(hb_meta, None),
        jobs=[_CastJob(conv_w_out, 0, 64, 512), _CastJob(ffn_w_gate, 1, 256, 256, norm_ffn[1]),
              _CastJob(ffn_w_up, 1, 256, 256, norm_ffn[1]), _CastJob(conv_w_in, 0, 256, 256, norm_mix[1])],
        row_split=8, name="ffn_gate_up")
    h, hb, rstd, h_meta, (w_down1,) = _matmul(
        act, [(w_down0, 0)], d, _epilogue_resid, _F32, bm=512, bn=512, resid=h, meta=(act_meta, h_meta),
        emit_stats=True, jobs=[_CastJob(ffn_w_down, 1, d_ff // 64, 512)], row_split=2, name="ffn_down")

    hb_meta, rstd_meta = _row_stats(h_meta)
    y = _conv_in(hb, rstd, hb_meta, rstd_meta, w_cin, conv_w[0], seq=seq, bm=2048, bn=256, row_split=8)
    h, hb, rstd, _, _ = _matmul(y, [(w_cout, 0)], d, _epilogue_resid, _F32, bm=1024, bn=512, resid=h,
                                emit_stats=True, row_split=4, name="conv_out")

    act, _, _ = _matmul(hb, [(w_gate1, 0), (w_up1, 0)], d_ff, _epilogue_swiglu, _BF16, bm=2048, bn=256,
                        scale=(rstd, None), row_split=8, name="ffn_gate_up")
    h, _, _ = _matmul(act, [(w_down1, 0)], d, _epilogue_resid, _F32, bm=512, bn=512, resid=h, row_split=2,
                      name="ffn_down")
    out = _rmsnorm(h, norm_final, _F32)
    return out.reshape(n_batch, seq, d)
```

```python
import functools
from typing import NamedTuple

import jax
import jax.numpy as jnp
from jax import lax
from jax.experimental import pallas as pl
from jax.experimental.pallas import tpu as pltpu

_D = 4096
_N_META = 16
_HEADS = 4
_DK = 512
_DV = 1024
_QK = _HEADS * _DK
_V = _HEADS * _DV
_RANK = 16
_GATE_TAU = 16.0
_CONV_WIDTH = 3
_EPS = 1e-6
_LOG2E = 1.4426950408889634

_LANES = 128
_SUBLANES = 8
_BF16_SUBLANES = 16
_V7X_SCOPED_VMEM_BYTES = 60000 * 1024

_F32 = jnp.float32
_BF16 = jnp.bfloat16


def _params(block_bytes, scratch_bytes=0, n_grid=2):
    need = 2 * sum(block_bytes) + scratch_bytes
    assert need <= _V7X_SCOPED_VMEM_BYTES, (need, _V7X_SCOPED_VMEM_BYTES)
    return pltpu.CompilerParams(
        dimension_semantics=("arbitrary",) * n_grid,
        vmem_limit_bytes=_V7X_SCOPED_VMEM_BYTES,
    )


class _CastJob(NamedTuple):
    src: jax.Array
    layer: int
    n_blocks: int
    tile_cols: int
    gain: jax.Array = None

    @property
    def block(self):
        rows, cols = self.src.shape[1:]
        assert rows % self.n_blocks == 0 and (rows // self.n_blocks) % _BF16_SUBLANES == 0
        assert cols % self.tile_cols == 0
        return rows // self.n_blocks, cols


def _job_plumbing(jobs, n_steps, flat_step):
    in_specs, args, out_specs, out_shapes, blocks, desc = [], [], [], [], [], []
    for job in jobs:
        assert job.n_blocks <= n_steps, (job.n_blocks, n_steps)
        rb, cols = job.block
        last, layer = job.n_blocks - 1, job.layer

        def blk(*g, last=last):
            return jnp.minimum(flat_step(*g), last)

        in_specs.append(pl.BlockSpec((1, rb, cols), lambda *g, blk=blk, layer=layer: (layer, blk(*g), 0)))
        args.append(job.src)
        if job.gain is not None:
            in_specs.append(pl.BlockSpec((rb, 1), lambda *g, blk=blk: (blk(*g), 0)))
            args.append(job.gain.reshape(-1, 1))
        n_tiles, tc = cols // job.tile_cols, job.tile_cols
        out_specs.append(pl.BlockSpec((n_tiles, rb, tc), lambda *g, blk=blk: (0, blk(*g), 0)))
        out_shapes.append(jax.ShapeDtypeStruct((n_tiles, job.src.shape[1], tc), _BF16))
        blocks += [rb * cols * 4, rb * cols * 2, rb * _LANES * 4]
        desc.append(job.gain is not None)
    return in_specs, args, out_specs, out_shapes, blocks, tuple(desc)


def _take_job_inputs(it, desc):
    return [(next(it), next(it) if has_gain else None) for has_gain in desc]


def _run_casts(in_refs, dst_refs):
    for (src, gain), dst in zip(in_refs, dst_refs):
        w = src[0]
        if gain is not None:
            w = w * gain[...]
        w = w.astype(_BF16)
        n_tiles, _, tc = dst.shape
        for t in range(n_tiles):
            dst[t] = w[:, t * tc:(t + 1) * tc]


def _cast_rows_kernel(src_ref, gain_ref, dst_ref, *, n_valid):
    w = src_ref[0] * gain_ref[...]
    if n_valid < w.shape[0]:
        w = jnp.where(lax.broadcasted_iota(jnp.int32, w.shape, 0) < n_valid, w, 0.0)
    dst_ref[0] = jnp.transpose(w).astype(_BF16)


def _cast_rows(src_t, layer, gain, row0, n_rows, tile, rb):
    src_rows, cols = src_t.shape[1:]
    n_valid = min(n_rows, src_rows - row0)
    n_tiles = -(-n_rows // tile)
    per_tile = tile // rb
    assert tile % rb == 0 and row0 % rb == 0 and (n_valid == n_tiles * tile or (n_tiles == 1 and rb == tile))
    r0 = row0 // rb
    return pl.pallas_call(
        functools.partial(_cast_rows_kernel, n_valid=n_valid),
        grid=(n_tiles * per_tile,),
        in_specs=[pl.BlockSpec((1, rb, cols), lambda i: (layer, r0 + i, 0)),
                  pl.BlockSpec((1, cols), lambda i: (0, 0))],
        out_specs=pl.BlockSpec((1, cols, rb), lambda i: (i // per_tile, 0, i % per_tile)),
        out_shape=jax.ShapeDtypeStruct((n_tiles, cols, tile), _BF16),
        compiler_params=_params([rb * cols * 4, rb * cols * 2, cols * 4], scratch_bytes=2 * rb * cols * 4, n_grid=1),
        name="weight_cast",
    )(src_t, gain.reshape(1, -1))


def _row_stats_kernel(x_ref, xb_ref, rstd_ref):
    x = x_ref[...]
    xb_ref[...] = x.astype(_BF16)
    rstd_ref[...] = lax.rsqrt(jnp.mean(x * x, axis=-1, keepdims=True) + _EPS)


def _row_stats(x, bm=256):
    m, d = x.shape
    bm = min(bm, m)
    assert m % bm == 0
    return pl.pallas_call(
        _row_stats_kernel,
        grid=(m // bm,),
        in_specs=[pl.BlockSpec((bm, d), lambda i: (i, 0))],
        out_specs=[pl.BlockSpec((bm, d), lambda i: (i, 0)), pl.BlockSpec((bm, 1), lambda i: (i, 0))],
        out_shape=[jax.ShapeDtypeStruct((m, d), _BF16), jax.ShapeDtypeStruct((m, 1), _F32)],
        compiler_params=_params([bm * d * 4, bm * d * 2, bm * _LANES * 4], scratch_bytes=2 * bm * d * 4, n_grid=1),
        name="row_stats",
    )(x)


def _rmsnorm_kernel(x_ref, g_ref, o_ref):
    x = x_ref[...]
    ms = jnp.mean(x * x, axis=-1, keepdims=True)
    o_ref[...] = (x * lax.rsqrt(ms + _EPS) * g_ref[...]).astype(o_ref.dtype)


def _rmsnorm(x, g, out_dtype, bm=256):
    m, d = x.shape
    bm = min(bm, m)
    assert m % bm == 0
    blocks = [bm * d * 4, d * 4, bm * d * jnp.dtype(out_dtype).itemsize]
    return pl.pallas_call(
        _rmsnorm_kernel,
        grid=(m // bm,),
        in_specs=[pl.BlockSpec((bm, d), lambda i: (i, 0)), pl.BlockSpec((1, d), lambda i: (0, 0))],
        out_specs=pl.BlockSpec((bm, d), lambda i: (i, 0)),
        out_shape=jax.ShapeDtypeStruct((m, d), out_dtype),
        compiler_params=_params(blocks, scratch_bytes=2 * bm * d * 4, n_grid=1),
        name="rmsnorm",
    )(x, g.reshape(1, d))


def _epilogue_cast(accs, resid):
    return accs[0]


def _epilogue_resid(accs, resid):
    return resid + accs[0]


def _epilogue_swiglu(accs, resid):
    gate, up = accs
    return gate * jax.nn.sigmoid(gate) * up


def _matmul_kernel(*refs, n_w, epilogue, has_resid, has_scale, has_meta, emit_stats, jobs, n_cols, row_split):
    it = iter(refs)
    a_ref = next(it)
    w_refs = [next(it) for _ in range(n_w)]
    r_ref = next(it) if has_resid else None
    s_ref = next(it) if has_scale else None
    am_ref = next(it) if has_meta else None
    rm_ref = next(it) if has_meta and has_resid else None
    sm_ref = next(it) if has_meta and has_scale else None
    job_in = _take_job_inputs(it, jobs)
    o_ref = next(it)
    ob_ref, rstd_ref = (next(it), next(it)) if emit_stats else (None, None)
    om_ref = next(it) if has_meta else None
    dst_refs = [next(it) for _ in jobs]
    ssq_scr = next(it) if emit_stats else None
    i, j = pl.program_id(0), pl.program_id(1)

    def result(a, resid, scale):
        accs = [jnp.dot(a, w[0], preferred_element_type=_F32) for w in w_refs]
        if scale is not None:
            accs = [acc * scale for acc in accs]
        return epilogue(accs, resid)

    if emit_stats:
        @pl.when(j == 0)
        def _reset():
            ssq_scr[...] = jnp.zeros(ssq_scr.shape, _F32)

    sub_rows = o_ref.shape[0] // row_split
    for h in range(row_split):
        rows = slice(h * sub_rows, (h + 1) * sub_rows)
        out = result(a_ref[rows, :], r_ref[rows, :] if has_resid else None, s_ref[rows, :] if has_scale else None)
        o_ref[rows, :] = out.astype(o_ref.dtype)
        if emit_stats:
            ob_ref[rows, :] = out.astype(_BF16)
            ssq = ssq_scr[rows, :] + jnp.sum(out * out, axis=-1, keepdims=True)
            ssq_scr[rows, :] = ssq
            rstd_ref[rows, :] = lax.rsqrt(ssq * (1.0 / n_cols) + _EPS)
        _run_casts(job_in[h::row_split], dst_refs[h::row_split])

    if has_meta:
        @pl.when(i == 0)
        def _meta_rows():
            out_m = result(am_ref[...], rm_ref[...] if has_resid else None, sm_ref[...] if has_scale else None)
            om_ref[0] = out_m.astype(om_ref.dtype)

        @pl.when(i != 0)
        def _meta_unused():
            om_ref[0] = jnp.zeros(om_ref.shape[1:], om_ref.dtype)


def _matmul(a, ws, n_out, epilogue, out_dtype, bm, bn, resid=None, scale=None, meta=None, emit_stats=False,
            jobs=(), row_split=1, name="matmul"):
    m, k = a.shape
    bm = min(bm, m)
    assert m % bm == 0 and n_out % bn == 0
    n_i, n_j = m // bm, n_out // bn
    out_bytes = jnp.dtype(out_dtype).itemsize
    has_resid, has_scale, has_meta = resid is not None, scale is not None, meta is not None

    in_specs = [pl.BlockSpec((bm, k), lambda i, j: (i, 0))]
    args = [a]
    blocks = [bm * k * 2, bm * bn * out_bytes]
    for w, off in ws:
        assert w.shape[1:] == (k, bn), (w.shape, k, bn)
        in_specs.append(pl.BlockSpec((1, k, bn), lambda i, j, off=off: (off + j, 0, 0)))
        args.append(w)
        blocks.append(k * bn * 2)
    if has_resid:
        in_specs.append(pl.BlockSpec((bm, bn), lambda i, j: (i, j)))
        args.append(resid)
        blocks.append(bm * bn * 4)
    if has_scale:
        in_specs.append(pl.BlockSpec((bm, 1), lambda i, j: (i, 0)))
        args.append(scale[0])
        blocks.append(bm * _LANES * 4)
    out_specs = [pl.BlockSpec((bm, bn), lambda i, j: (i, j))]
    out_shapes = [jax.ShapeDtypeStruct((m, n_out), out_dtype)]
    scratch = []
    if emit_stats:
        out_specs += [pl.BlockSpec((bm, bn), lambda i, j: (i, j)), pl.BlockSpec((bm, 1), lambda i, j: (i, 0))]
        out_shapes += [jax.ShapeDtypeStruct((m, n_out), _BF16), jax.ShapeDtypeStruct((m, 1), _F32)]
        blocks += [bm * bn * 2, bm * _LANES * 4]
        scratch.append(pltpu.VMEM((bm, 1), _F32))
    if has_meta:
        a_meta, r_meta = meta
        in_specs.append(pl.BlockSpec((_N_META, k), lambda i, j: (0, 0)))
        args.append(a_meta)
        blocks.append(_N_META * k * 2)
        if has_resid:
            in_specs.append(pl.BlockSpec((_N_META, bn), lambda i, j: (0, j)))
            args.append(r_meta)
        if has_scale:
            in_specs.append(pl.BlockSpec((_N_META, 1), lambda i, j: (0, 0)))
            args.append(scale[1])
        out_specs.append(pl.BlockSpec((1, _N_META, bn), lambda i, j: (i, 0, j)))
        out_shapes.append(jax.ShapeDtypeStruct((n_i, _N_META, n_out), out_dtype))
    j_in, j_args, j_out, j_shapes, j_blocks, j_desc = _job_plumbing(jobs, n_i * n_j, lambda i, j: i * n_j + j)

    kernel = functools.partial(_matmul_kernel, n_w=len(ws), epilogue=epilogue, has_resid=has_resid,
                               has_scale=has_scale, has_meta=has_meta, emit_stats=emit_stats, jobs=j_desc,
                               n_cols=n_out, row_split=row_split)
    outs = pl.pallas_call(
        kernel,
        grid=(n_i, n_j),
        in_specs=in_specs + j_in,
        out_specs=out_specs + j_out,
        out_shape=out_shapes + j_shapes,
        scratch_shapes=scratch,
        compiler_params=_params(blocks + j_blocks,
                                scratch_bytes=(len(ws) + 1 + emit_stats) * (bm // row_split) * bn * 4),
        name=name,
    )(*args, *j_args)
    n_main = 3 if emit_stats else 1
    main = list(outs[:n_main])
    out_meta = outs[n_main][0] if has_meta else None
    return (*main, out_meta, list(outs[n_main + has_meta:]))


def _cumsum_rows(tri, g):
    g1 = g.astype(_BF16)
    r1 = g - g1.astype(_F32)
    g2 = r1.astype(_BF16)
    g3 = (r1 - g2.astype(_F32)).astype(_BF16)
    out = jnp.dot(tri, g1, preferred_element_type=_F32)
    out = out + jnp.dot(tri, g2, preferred_element_type=_F32)
    return out + jnp.dot(tri, g3, preferred_element_type=_F32)


def _gla_kernel(q_ref, k_ref, v_ref, r_ref, al_ref, wa2_ref, ba_ref, hnorm_ref, s0_ref, o_ref, sfin_ref,
                s_scr, b_scr, bnext_scr, o_scr, *, chunk, n_chunks, sub):
    step = pl.program_id(2)

    @pl.when(step == 0)
    def _load_state():
        s_scr[...] = s0_ref[0]

    c = chunk
    state_tile = 2 * _LANES
    n_sub = c // sub
    tri = (lax.broadcasted_iota(jnp.int32, (c, c), 0) >= lax.broadcasted_iota(jnp.int32, (c, c), 1)).astype(_BF16)
    sub_row = lax.broadcasted_iota(jnp.int32, (sub, c), 0)
    sub_col = lax.broadcasted_iota(jnp.int32, (sub, c), 1)

    q_scale = _DK ** -0.5
    trans_b = (((1,), (1,)), ((), ()))

    def gate_of_chunk(ci):
        rows = pl.ds(ci * c if isinstance(ci, int) else pl.multiple_of(ci * c, c), c)
        return jnp.dot(al_ref[rows, :].astype(_BF16), wa2_ref[...], preferred_element_type=_F32) + ba_ref[...]

    def decay_of_gate(x):
        g = (jnp.minimum(x, 0.0) - jnp.log(1.0 + jnp.exp(-jnp.abs(x)))) * (_LOG2E / _GATE_TAU)
        return _cumsum_rows(tri, g)

    def one_chunk(ci, carry):
        base = pl.multiple_of(ci * c, c)
        rows = pl.ds(base, c)
        b_scr[...] = bnext_scr[...]
        b_last = b_scr[c - 1:c, :]
        x_next = gate_of_chunk(jnp.minimum(ci + 1, n_chunks - 1))

        q_dec = (q_ref[rows, :].astype(_F32) * q_scale * jnp.exp2(b_scr[...])).astype(_BF16)
        o_scr[...] = jnp.dot(q_dec, s_scr[...].astype(_BF16), preferred_element_type=_F32)

        diag, off = [], [None]
        for blk in range(n_sub):
            r0 = blk * sub
            b_b = b_scr[r0:r0 + sub, :]
            q_b = q_ref[pl.ds(base + r0, sub), :].astype(_F32) * q_scale
            stacked = jnp.concatenate(
                [q_b * jnp.exp2(b_b - b_scr[r0 + j:r0 + j + 1, :]) for j in range(sub)], axis=0).astype(_BF16)
            diag.append(lax.dot_general(stacked, k_ref[rows, :], trans_b, preferred_element_type=_F32))
            if blk > 0:
                p = b_scr[r0 - 1:r0, :]
                q_o = (q_b * jnp.exp2(b_b - p)).astype(_BF16)
                n_keys = -(-r0 // _BF16_SUBLANES) * _BF16_SUBLANES
                k_o = (k_ref[pl.ds(base, n_keys), :].astype(_F32) * jnp.exp2(p - b_scr[0:n_keys, :])).astype(_BF16)
                if n_keys < c:
                    k_o = jnp.concatenate([k_o, jnp.zeros((c - n_keys, _DK), _BF16)], axis=0)
                off.append(lax.dot_general(q_o, k_o, trans_b, preferred_element_type=_F32))

        b_next = decay_of_gate(x_next)

        tiles = []
        for blk in range(n_sub):
            r0 = blk * sub
            sc = jnp.zeros((sub, c), _F32)
            for j in range(sub):
                sc = jnp.where(sub_col == r0 + j, diag[blk][j * sub:(j + 1) * sub], sc)
            sc = jnp.where(sub_row + r0 >= sub_col, sc, 0.0)
            if blk > 0:
                sc = sc + jnp.where(sub_col < r0, off[blk], 0.0)
            tiles.append(sc)
        scores = jnp.concatenate(tiles, axis=0) if n_sub > 1 else tiles[0]
        o_intra = jnp.dot(scores.astype(_BF16), v_ref[rows, :], preferred_element_type=_F32)
        bnext_scr[...] = b_next

        k_dec = (k_ref[rows, :].astype(_F32) * jnp.exp2(b_last - b_scr[...])).astype(_BF16)
        decay_col = jnp.transpose(jnp.broadcast_to(jnp.exp2(b_last), (_LANES, _DK)))
        for t in range(_DV // state_tile):
            cols = slice(t * state_tile, (t + 1) * state_tile)
            upd = lax.dot_general(k_dec, v_ref[rows, cols], (((0,), (0,)), ((), ())), preferred_element_type=_F32)
            s_scr[:, cols] = s_scr[:, cols] * jnp.tile(decay_col, (1, state_tile // _LANES)) + upd

        o = o_scr[...] + o_intra
        ms = jnp.mean(o * o, axis=-1, keepdims=True)
        o_n = o * lax.rsqrt(ms + _EPS) * hnorm_ref[...]
        r = r_ref[rows, :].astype(_F32)
        o_ref[rows, :] = (o_n * (r * jax.nn.sigmoid(r))).astype(o_ref.dtype)
        return carry

    bnext_scr[...] = decay_of_gate(gate_of_chunk(0))
    lax.fori_loop(0, n_chunks, one_chunk, 0)

    @pl.when(step == pl.num_programs(2) - 1)
    def _store_state():
        sfin_ref[0, 0] = s_scr[...]


def _gla(proj, a_low, wa2, ba, hnorm, s0, *, n_batch, seq, chunk, rows_per_step):
    m = proj.shape[0]
    assert m == n_batch * seq and seq % rows_per_step == 0 and rows_per_step % chunk == 0
    steps = seq // rows_per_step
    r = rows_per_step
    kv = _QK // _DV

    def row(b, h, i):
        return b * steps + i

    kernel = functools.partial(_gla_kernel, chunk=chunk, n_chunks=r // chunk, sub=min(chunk, 8))
    blocks = [r * _DK * 2, r * _DK * 2, r * _DV * 2, r * _DV * 2, r * _LANES * 4, _LANES * _DK * 2,
              _DK * 4, _DV * 4, _DK * _DV * 4, r * _DV * 2, _DK * _DV * 4]
    outs = pl.pallas_call(
        kernel,
        grid=(n_batch, _HEADS, steps),
        in_specs=[
            pl.BlockSpec((r, _DK), lambda b, h, i: (row(b, h, i), h)),
            pl.BlockSpec((r, _DK), lambda b, h, i: (row(b, h, i), _HEADS + h)),
            pl.BlockSpec((r, _DV), lambda b, h, i: (row(b, h, i), 2 * kv + h)),
            pl.BlockSpec((r, _DV), lambda b, h, i: (row(b, h, i), 2 * kv + _HEADS + h)),
            pl.BlockSpec((r, _LANES), lambda b, h, i: (row(b, h, i), 0)),
            pl.BlockSpec((_LANES, _DK), lambda b, h, i: (0, h)),
            pl.BlockSpec((1, _DK), lambda b, h, i: (0, h)),
            pl.BlockSpec((1, _DV), lambda b, h, i: (0, 0)),
            pl.BlockSpec((1, _DK, _DV), lambda b, h, i: (h, 0, 0)),
        ],
        out_specs=[
            pl.BlockSpec((r, _DV), lambda b, h, i: (row(b, h, i), h)),
            pl.BlockSpec((1, 1, _DK, _DV), lambda b, h, i: (b, h, 0, 0)),
        ],
        out_shape=[
            jax.ShapeDtypeStruct((m, _V), _BF16),
            jax.ShapeDtypeStruct((n_batch, _HEADS, _DK, _DV), _F32),
        ],
        scratch_shapes=[pltpu.VMEM((_DK, _DV), _F32), pltpu.VMEM((chunk, _DK), _F32), pltpu.VMEM((chunk, _DK), _F32),
                        pltpu.VMEM((chunk, _DV), _F32)],
        compiler_params=_params(blocks, scratch_bytes=4 * _DK * _DV * 4, n_grid=3),
        name="gla",
    )(proj, proj, proj, proj, a_low, wa2, ba.reshape(1, _QK), hnorm.reshape(1, _DV), s0)
    return outs[0], outs[1]


def _conv_in_kernel(a_ref, wb_ref, wc_ref, wu_ref, s_ref, am_ref, sm_ref, cw_ref, o_ref, tail_scr, meta_scr,
                    *, blocks_per_seq, row_split):
    i, j = pl.program_id(0), pl.program_id(1)

    @pl.when(i == 0)
    def _meta_rows():
        am, sm = am_ref[...], sm_ref[...]
        z_meta = ((jnp.dot(am, wc_ref[0], preferred_element_type=_F32) * sm)
                  * (jnp.dot(am, wu_ref[0], preferred_element_type=_F32) * sm))
        meta_scr[j] = z_meta[_N_META - _SUBLANES:]
        tail_scr[j] = z_meta[_N_META - _SUBLANES:]

    prev = jnp.where(i % blocks_per_seq == 0, meta_scr[j], tail_scr[j])
    cw = cw_ref[...]
    sub_rows = o_ref.shape[0] // row_split
    rid = lax.broadcasted_iota(jnp.int32, (sub_rows, o_ref.shape[1]), 0)
    for h in range(row_split):
        rows = slice(h * sub_rows, (h + 1) * sub_rows)
        a, s = a_ref[rows, :], s_ref[rows, :]
        bg = jnp.dot(a, wb_ref[0], preferred_element_type=_F32) * s
        z = ((jnp.dot(a, wc_ref[0], preferred_element_type=_F32) * s)
             * (jnp.dot(a, wu_ref[0], preferred_element_type=_F32) * s))
        z_m1, z_m2 = prev[_SUBLANES - 1:_SUBLANES], prev[_SUBLANES - 2:_SUBLANES - 1]
        z1 = jnp.where(rid == 0, z_m1, pltpu.roll(z, 1, 0))
        z2 = jnp.where(rid == 0, z_m2, jnp.where(rid == 1, z_m1, pltpu.roll(z, 2, 0)))
        conv = cw[2:3] * z + cw[0:1] * z2 + cw[1:2] * z1
        o_ref[rows, :] = (bg * conv).astype(o_ref.dtype)
        prev = z[sub_rows - _SUBLANES:]
    tail_scr[j] = prev


def _conv_in(a, rstd, a_meta, rstd_meta, w, conv_w, *, seq, bm, bn, row_split):
    m, k = a.shape
    assert m % bm == 0 and seq % bm == 0 and _D % bn == 0
    n_i, n_j = m // bm, _D // bn
    kernel = functools.partial(_conv_in_kernel, blocks_per_seq=seq // bm, row_split=row_split)
    blocks = [bm * k * 2, 3 * k * bn * 2, bm * _LANES * 4, _N_META * k * 2, _CONV_WIDTH * bn * 4, bm * bn * 2]
    return pl.pallas_call(
        kernel,
        grid=(n_i, n_j),
        in_specs=[
            pl.BlockSpec((bm, k), lambda i, j: (i, 0)),
            pl.BlockSpec((1, k, bn), lambda i, j: (j, 0, 0)),
            pl.BlockSpec((1, k, bn), lambda i, j: (n_j + j, 0, 0)),
            pl.BlockSpec((1, k, bn), lambda i, j: (2 * n_j + j, 0, 0)),
            pl.BlockSpec((bm, 1), lambda i, j: (i, 0)),
            pl.BlockSpec((_N_META, k), lambda i, j: (0, 0)),
            pl.BlockSpec((_N_META, 1), lambda i, j: (0, 0)),
            pl.BlockSpec((_CONV_WIDTH, bn), lambda i, j: (0, j)),
        ],
        out_specs=pl.BlockSpec((bm, bn), lambda i, j: (i, j)),
        out_shape=jax.ShapeDtypeStruct((m, _D), _BF16),
        scratch_shapes=[pltpu.VMEM((n_j, _SUBLANES, bn), _F32), pltpu.VMEM((n_j, _SUBLANES, bn), _F32)],
        compiler_params=_params(blocks, scratch_bytes=8 * (bm // row_split) * bn * 4),
        name="conv_in",
    )(a, w, w, w, rstd, a_meta, rstd_meta, conv_w)


def kernel(x, meta, norm_mix, norm_ffn, gla_w_in, gla_w_a2, gla_b_a, gla_head_norm, gla_w_out,
           conv_w_in, conv_w, conv_w_out, ffn_w_gate, ffn_w_up, ffn_w_down, norm_final):
    n_batch, seq, d = x.shape
    xm = x.reshape(n_batch * seq, d)
    xmeta = meta.astype(x.dtype)
    n_qkvr = 2 * _QK + 2 * _V
    d_ff = ffn_w_gate.shape[2]
    assert d_ff % 256 == 0


    w_in_t = jnp.transpose(gla_w_in, (0, 2, 1))
    w_qkvr = _cast_rows(w_in_t, 0, norm_mix[0], 0, n_qkvr, tile=1024, rb=512)
    w_a1 = _cast_rows(w_in_t, 0, norm_mix[0], n_qkvr, _RANK, tile=_LANES, rb=_LANES)
    w_a2 = jnp.pad(gla_w_a2[0], ((0, _LANES - _RANK), (0, 0))).astype(_BF16)

    hb, rstd = _row_stats(xm)
    hb_meta, rstd_meta = _row_stats(xmeta)
    proj, proj_meta, (w_out0, w_gate0, w_up0) = _matmul(
        hb, [(w_qkvr, 0)], n_qkvr, _epilogue_cast, _BF16, bm=1024, bn=1024, scale=(rstd, rstd_meta),
        meta=(hb_meta, None),
        jobs=[_CastJob(gla_w_out, 0, 64, 512), _CastJob(ffn_w_gate, 0, 128, 256, norm_ffn[0]),
              _CastJob(ffn_w_up, 0, 128, 256, norm_ffn[0])],
        row_split=4, name="gla_in")
    a_low, a_low_meta, _ = _matmul(hb, [(w_a1, 0)], _LANES, _epilogue_cast, _F32, bm=1024, bn=_LANES,
                                   scale=(rstd, rstd_meta), meta=(hb_meta, None), name="gla_gate_in")

    zero_state = jnp.zeros((_HEADS, _DK, _DV), _F32)
    o_meta, s_meta = _gla(proj_meta, a_low_meta, w_a2, gla_b_a[0], gla_head_norm[0], zero_state,
                          n_batch=1, seq=_N_META, chunk=_N_META, rows_per_step=_N_META)
    o_x, _ = _gla(proj, a_low, w_a2, gla_b_a[0], gla_head_norm[0], s_meta[0],
                  n_batch=n_batch, seq=seq, chunk=128, rows_per_step=1024)
    h, hb, rstd, h_meta, (w_down0,) = _matmul(
        o_x, [(w_out0, 0)], d, _epilogue_resid, _F32, bm=1024, bn=512, resid=xm, meta=(o_meta, xmeta),
        emit_stats=True, jobs=[_CastJob(ffn_w_down, 0, d_ff // 128, 512)], row_split=4, name="gla_out")

    hb_meta, rstd_meta = _row_stats(h_meta)
    act, act_meta, (w_cout, w_gate1, w_up1, w_cin) = _matmul(
        hb, [(w_gate0, 0), (w_up0, 0)], d_ff, _epilogue_swiglu, _BF16, bm=2048, bn=256, scale=(rstd, rstd_meta),
        meta=(hb_meta, None),
        jobs=[_CastJob(conv_w_out, 0, 64, 512), _CastJob(ffn_w_gate, 1, 256, 256, norm_ffn[1]),
              _CastJob(ffn_w_up, 1, 256, 256, norm_ffn[1]), _CastJob(conv_w_in, 0, 256, 256, norm_mix[1])],
        row_split=8, name="ffn_gate_up")
    h, hb, rstd, h_meta, (w_down1,) = _matmul(
        act, [(w_down0, 0)], d, _epilogue_resid, _F32, bm=512, bn=512, resid=h, meta=(act_meta, h_meta),
        emit_stats=True, jobs=[_CastJob(ffn_w_down, 1, d_ff // 64, 512)], row_split=2, name="ffn_down")

    hb_meta, rstd_meta = _row_stats(h_meta)
    y = _conv_in(hb, rstd, hb_meta, rstd_meta, w_cin, conv_w[0], seq=seq, bm=2048, bn=256, row_split=8)
    h, hb, rstd, _, _ = _matmul(y, [(w_cout, 0)], d, _epilogue_resid, _F32, bm=1024, bn=512, resid=h,
                                emit_stats=True, row_split=4, name="conv_out")

    act, _, _ = _matmul(hb, [(w_gate1, 0), (w_up1, 0)], d_ff, _epilogue_swiglu, _BF16, bm=2048, bn=256,
                        scale=(rstd, None), row_split=8, name="ffn_gate_up")
    h, _, _ = _matmul(act, [(w_down1, 0)], d, _epilogue_resid, _F32, bm=512, bn=512, resid=h, row_split=2,
                      name="ffn_down")
    out = _rmsnorm(h, norm_final, _F32)
    return out.reshape(n_batch, seq, d)
```

```python
import functools
from typing import NamedTuple

import jax
import jax.numpy as jnp
from jax import lax
from jax.experimental import pallas as pl
from jax.experimental.pallas import tpu as pltpu

_D = 4096
_N_META = 16
_HEADS = 4
_DK = 512
_DV = 1024
_QK = _HEADS * _DK
_V = _HEADS * _DV
_RANK = 16
_GATE_TAU = 16.0
_CONV_WIDTH = 3
_EPS = 1e-6
_LOG2E = 1.4426950408889634

_LANES = 128
_SUBLANES = 8
_BF16_SUBLANES = 16
_V7X_SCOPED_VMEM_BYTES = 60000 * 1024

_F32 = jnp.float32
_BF16 = jnp.bfloat16


def _params(block_bytes, scratch_bytes=0, n_grid=2):
    need = 2 * sum(block_bytes) + scratch_bytes
    assert need <= _V7X_SCOPED_VMEM_BYTES, (need, _V7X_SCOPED_VMEM_BYTES)
    return pltpu.CompilerParams(
        dimension_semantics=("arbitrary",) * n_grid,
        vmem_limit_bytes=_V7X_SCOPED_VMEM_BYTES,
    )


class _CastJob(NamedTuple):
    src: jax.Array
    layer: int
    n_blocks: int
    tile_cols: int
    gain: jax.Array = None

    @property
    def block(self):
        rows, cols = self.src.shape[1:]
        assert rows % self.n_blocks == 0 and (rows // self.n_blocks) % _BF16_SUBLANES == 0
        assert cols % self.tile_cols == 0
        return rows // self.n_blocks, cols


def _job_plumbing(jobs, n_steps, flat_step):
    in_specs, args, out_specs, out_shapes, blocks, desc = [], [], [], [], [], []
    for job in jobs:
        assert job.n_blocks <= n_steps, (job.n_blocks, n_steps)
        rb, cols = job.block
        last, layer = job.n_blocks - 1, job.layer

        def blk(*g, last=last):
            return jnp.minimum(flat_step(*g), last)

        in_specs.append(pl.BlockSpec((1, rb, cols), lambda *g, blk=blk, layer=layer: (layer, blk(*g), 0)))
        args.append(job.src)
        if job.gain is not None:
            in_specs.append(pl.BlockSpec((rb, 1), lambda *g, blk=blk: (blk(*g), 0)))
            args.append(job.gain.reshape(-1, 1))
        n_tiles, tc = cols // job.tile_cols, job.tile_cols
        out_specs.append(pl.BlockSpec((n_tiles, rb, tc), lambda *g, blk=blk: (0, blk(*g), 0)))
        out_shapes.append(jax.ShapeDtypeStruct((n_tiles, job.src.shape[1], tc), _BF16))
        blocks += [rb * cols * 4, rb * cols * 2, rb * _LANES * 4]
        desc.append(job.gain is not None)
    return in_specs, args, out_specs, out_shapes, blocks, tuple(desc)


def _take_job_inputs(it, desc):
    return [(next(it), next(it) if has_gain else None) for has_gain in desc]


def _run_casts(in_refs, dst_refs):
    for (src, gain), dst in zip(in_refs, dst_refs):
        w = src[0]
        if gain is not None:
            w = w * gain[...]
        w = w.astype(_BF16)
        n_tiles, _, tc = dst.shape
        for t in range(n_tiles):
            dst[t] = w[:, t * tc:(t + 1) * tc]


def _cast_rows_kernel(src_ref, gain_ref, dst_ref, *, n_valid):
    w = src_ref[0] * gain_ref[...]
    if n_valid < w.shape[0]:
        w = jnp.where(lax.broadcasted_iota(jnp.int32, w.shape, 0) < n_valid, w, 0.0)
    dst_ref[0] = jnp.transpose(w).astype(_BF16)


def _cast_rows(src_t, layer, gain, row0, n_rows, tile, rb):
    src_rows, cols = src_t.shape[1:]
    n_valid = min(n_rows, src_rows - row0)
    n_tiles = -(-n_rows // tile)
    per_tile = tile // rb
    assert tile % rb == 0 and row0 % rb == 0 and (n_valid == n_tiles * tile or (n_tiles == 1 and rb == tile))
    r0 = row0 // rb
    return pl.pallas_call(
        functools.partial(_cast_rows_kernel, n_valid=n_valid),
        grid=(n_tiles * per_tile,),
        in_specs=[pl.BlockSpec((1, rb, cols), lambda i: (layer, r0 + i, 0)),
                  pl.BlockSpec((1, cols), lambda i: (0, 0))],
        out_specs=pl.BlockSpec((1, cols, rb), lambda i: (i // per_tile, 0, i % per_tile)),
        out_shape=jax.ShapeDtypeStruct((n_tiles, cols, tile), _BF16),
        compiler_params=_params([rb * cols * 4, rb * cols * 2, cols * 4], scratch_bytes=2 * rb * cols * 4, n_grid=1),
        name="weight_cast",
    )(src_t, gain.reshape(1, -1))


def _row_stats_kernel(x_ref, xb_ref, rstd_ref):
    x = x_ref[...]
    xb_ref[...] = x.astype(_BF16)
    rstd_ref[...] = lax.rsqrt(jnp.mean(x * x, axis=-1, keepdims=True) + _EPS)


def _row_stats(x, bm=512):
    m, d = x.shape
    bm = min(bm, m)
    assert m % bm == 0
    return pl.pallas_call(
        _row_stats_kernel,
        grid=(m // bm,),
        in_specs=[pl.BlockSpec((bm, d), lambda i: (i, 0))],
        out_specs=[pl.BlockSpec((bm, d), lambda i: (i, 0)), pl.BlockSpec((bm, 1), lambda i: (i, 0))],
        out_shape=[jax.ShapeDtypeStruct((m, d), _BF16), jax.ShapeDtypeStruct((m, 1), _F32)],
        compiler_params=_params([bm * d * 4, bm * d * 2, bm * _LANES * 4], scratch_bytes=2 * bm * d * 4, n_grid=1),
        name="row_stats",
    )(x)


def _rmsnorm_kernel(x_ref, g_ref, o_ref):
    x = x_ref[...]
    ms = jnp.mean(x * x, axis=-1, keepdims=True)
    o_ref[...] = (x * lax.rsqrt(ms + _EPS) * g_ref[...]).astype(o_ref.dtype)


def _rmsnorm(x, g, out_dtype, bm=512):
    m, d = x.shape
    bm = min(bm, m)
    assert m % bm == 0
    blocks = [bm * d * 4, d * 4, bm * d * jnp.dtype(out_dtype).itemsize]
    return pl.pallas_call(
        _rmsnorm_kernel,
        grid=(m // bm,),
        in_specs=[pl.BlockSpec((bm, d), lambda i: (i, 0)), pl.BlockSpec((1, d), lambda i: (0, 0))],
        out_specs=pl.BlockSpec((bm, d), lambda i: (i, 0)),
        out_shape=jax.ShapeDtypeStruct((m, d), out_dtype),
        compiler_params=_params(blocks, scratch_bytes=2 * bm * d * 4, n_grid=1),
        name="rmsnorm",
    )(x, g.reshape(1, d))


def _epilogue_cast(accs, resid):
    return accs[0]


def _epilogue_resid(accs, resid):
    return resid + accs[0]


def _epilogue_swiglu(accs, resid):
    gate, up = accs
    return gate * jax.nn.sigmoid(gate) * up


def _matmul_kernel(*refs, n_w, epilogue, has_resid, has_scale, has_meta, emit_stats, jobs, n_cols, row_split):
    it = iter(refs)
    a_ref = next(it)
    w_refs = [next(it) for _ in range(n_w)]
    r_ref = next(it) if has_resid else None
    s_ref = next(it) if has_scale else None
    am_ref = next(it) if has_meta else None
    rm_ref = next(it) if has_meta and has_resid else None
    sm_ref = next(it) if has_meta and has_scale else None
    job_in = _take_job_inputs(it, jobs)
    o_ref = next(it)
    ob_ref, rstd_ref = (next(it), next(it)) if emit_stats else (None, None)
    om_ref = next(it) if has_meta else None
    dst_refs = [next(it) for _ in jobs]
    ssq_scr = next(it) if emit_stats else None
    i, j = pl.program_id(0), pl.program_id(1)

    def result(a, resid, scale):
        accs = [jnp.dot(a, w[0], preferred_element_type=_F32) for w in w_refs]
        if scale is not None:
            accs = [acc * scale for acc in accs]
        return epilogue(accs, resid)

    if emit_stats:
        @pl.when(j == 0)
        def _reset():
            ssq_scr[...] = jnp.zeros(ssq_scr.shape, _F32)

    sub_rows = o_ref.shape[0] // row_split
    for h in range(row_split):
        rows = slice(h * sub_rows, (h + 1) * sub_rows)
        out = result(a_ref[rows, :], r_ref[rows, :] if has_resid else None, s_ref[rows, :] if has_scale else None)
        o_ref[rows, :] = out.astype(o_ref.dtype)
        if emit_stats:
            ob_ref[rows, :] = out.astype(_BF16)
            ssq = ssq_scr[rows, :] + jnp.sum(out * out, axis=-1, keepdims=True)
            ssq_scr[rows, :] = ssq
            rstd_ref[rows, :] = lax.rsqrt(ssq * (1.0 / n_cols) + _EPS)
        _run_casts(job_in[h::row_split], dst_refs[h::row_split])

    if has_meta:
        @pl.when(i == 0)
        def _meta_rows():
            out_m = result(am_ref[...], rm_ref[...] if has_resid else None, sm_ref[...] if has_scale else None)
            om_ref[0] = out_m.astype(om_ref.dtype)

        @pl.when(i != 0)
        def _meta_unused():
            om_ref[0] = jnp.zeros(om_ref.shape[1:], om_ref.dtype)


def _matmul(a, ws, n_out, epilogue, out_dtype, bm, bn, resid=None, scale=None, meta=None, emit_stats=False,
            jobs=(), row_split=1, name="matmul"):
    m, k = a.shape
    bm = min(bm, m)
    assert m % bm == 0 and n_out % bn == 0
    n_i, n_j = m // bm, n_out // bn
    out_bytes = jnp.dtype(out_dtype).itemsize
    has_resid, has_scale, has_meta = resid is not None, scale is not None, meta is not None

    in_specs = [pl.BlockSpec((bm, k), lambda i, j: (i, 0))]
    args = [a]
    blocks = [bm * k * 2, bm * bn * out_bytes]
    for w, off in ws:
        assert w.shape[1:] == (k, bn), (w.shape, k, bn)
        in_specs.append(pl.BlockSpec((1, k, bn), lambda i, j, off=off: (off + j, 0, 0)))
        args.append(w)
        blocks.append(k * bn * 2)
    if has_resid:
        in_specs.append(pl.BlockSpec((bm, bn), lambda i, j: (i, j)))
        args.append(resid)
        blocks.append(bm * bn * 4)
    if has_scale:
        in_specs.append(pl.BlockSpec((bm, 1), lambda i, j: (i, 0)))
        args.append(scale[0])
        blocks.append(bm * _LANES * 4)
    out_specs = [pl.BlockSpec((bm, bn), lambda i, j: (i, j))]
    out_shapes = [jax.ShapeDtypeStruct((m, n_out), out_dtype)]
    scratch = []
    if emit_stats:
        out_specs += [pl.BlockSpec((bm, bn), lambda i, j: (i, j)), pl.BlockSpec((bm, 1), lambda i, j: (i, 0))]
        out_shapes += [jax.ShapeDtypeStruct((m, n_out), _BF16), jax.ShapeDtypeStruct((m, 1), _F32)]
        blocks += [bm * bn * 2, bm * _LANES * 4]
        scratch.append(pltpu.VMEM((bm, 1), _F32))
    if has_meta:
        a_meta, r_meta = meta
        in_specs.append(pl.BlockSpec((_N_META, k), lambda i, j: (0, 0)))
        args.append(a_meta)
        blocks.append(_N_META * k * 2)
        if has_resid:
            in_specs.append(pl.BlockSpec((_N_META, bn), lambda i, j: (0, j)))
            args.append(r_meta)
        if has_scale:
            in_specs.append(pl.BlockSpec((_N_META, 1), lambda i, j: (0, 0)))
            args.append(scale[1])
        out_specs.append(pl.BlockSpec((1, _N_META, bn), lambda i, j: (i, 0, j)))
        out_shapes.append(jax.ShapeDtypeStruct((n_i, _N_META, n_out), out_dtype))
    j_in, j_args, j_out, j_shapes, j_blocks, j_desc = _job_plumbing(jobs, n_i * n_j, lambda i, j: i * n_j + j)

    kernel = functools.partial(_matmul_kernel, n_w=len(ws), epilogue=epilogue, has_resid=has_resid,
                               has_scale=has_scale, has_meta=has_meta, emit_stats=emit_stats, jobs=j_desc,
                               n_cols=n_out, row_split=row_split)
    outs = pl.pallas_call(
        kernel,
        grid=(n_i, n_j),
        in_specs=in_specs + j_in,
        out_specs=out_specs + j_out,
        out_shape=out_shapes + j_shapes,
        scratch_shapes=scratch,
        compiler_params=_params(blocks + j_blocks,
                                scratch_bytes=(len(ws) + 1 + emit_stats) * (bm // row_split) * bn * 4),
        name=name,
    )(*args, *j_args)
    n_main = 3 if emit_stats else 1
    main = list(outs[:n_main])
    out_meta = outs[n_main][0] if has_meta else None
    return (*main, out_meta, list(outs[n_main + has_meta:]))


def _cumsum_rows(tri, g):
    g1 = g.astype(_BF16)
    r1 = g - g1.astype(_F32)
    g2 = r1.astype(_BF16)
    g3 = (r1 - g2.astype(_F32)).astype(_BF16)
    out = jnp.dot(tri, g1, preferred_element_type=_F32)
    out = out + jnp.dot(tri, g2, preferred_element_type=_F32)
    return out + jnp.dot(tri, g3, preferred_element_type=_F32)


def _gla_kernel(q_ref, k_ref, v_ref, r_ref, al_ref, wa2_ref, ba_ref, hnorm_ref, s0_ref, o_ref, sfin_ref,
                s_scr, b_scr, bnext_scr, o_scr, *, chunk, n_chunks, sub):
    step = pl.program_id(2)

    @pl.when(step == 0)
    def _load_state():
        s_scr[...] = s0_ref[0]

    c = chunk
    state_tile = 2 * _LANES
    n_sub = c // sub
    tri = (lax.broadcasted_iota(jnp.int32, (c, c), 0) >= lax.broadcasted_iota(jnp.int32, (c, c), 1)).astype(_BF16)
    sub_row = lax.broadcasted_iota(jnp.int32, (sub, c), 0)
    sub_col = lax.broadcasted_iota(jnp.int32, (sub, c), 1)

    q_scale = _DK ** -0.5
    trans_b = (((1,), (1,)), ((), ()))

    def gate_of_chunk(ci):
        rows = pl.ds(ci * c if isinstance(ci, int) else pl.multiple_of(ci * c, c), c)
        return jnp.dot(al_ref[rows, :].astype(_BF16), wa2_ref[...], preferred_element_type=_F32) + ba_ref[...]

    def decay_of_gate(x):
        g = (jnp.minimum(x, 0.0) - jnp.log(1.0 + jnp.exp(-jnp.abs(x)))) * (_LOG2E / _GATE_TAU)
        return _cumsum_rows(tri, g)

    def one_chunk(ci, carry):
        base = pl.multiple_of(ci * c, c)
        rows = pl.ds(base, c)
        b_scr[...] = bnext_scr[...]
        b_last = b_scr[c - 1:c, :]
        x_next = gate_of_chunk(jnp.minimum(ci + 1, n_chunks - 1))

        q_dec = (q_ref[rows, :].astype(_F32) * q_scale * jnp.exp2(b_scr[...])).astype(_BF16)
        o_scr[...] = jnp.dot(q_dec, s_scr[...].astype(_BF16), preferred_element_type=_F32)

        diag, off = [], [None]
        for blk in range(n_sub):
            r0 = blk * sub
            b_b = b_scr[r0:r0 + sub, :]
            q_b = q_ref[pl.ds(base + r0, sub), :].astype(_F32) * q_scale
            stacked = jnp.concatenate(
                [q_b * jnp.exp2(b_b - b_scr[r0 + j:r0 + j + 1, :]) for j in range(sub)], axis=0).astype(_BF16)
            diag.append(lax.dot_general(stacked, k_ref[rows, :], trans_b, preferred_element_type=_F32))
            if blk > 0:
                p = b_scr[r0 - 1:r0, :]
                q_o = (q_b * jnp.exp2(b_b - p)).astype(_BF16)
                n_keys = -(-r0 // _BF16_SUBLANES) * _BF16_SUBLANES
                k_o = (k_ref[pl.ds(base, n_keys), :].astype(_F32) * jnp.exp2(p - b_scr[0:n_keys, :])).astype(_BF16)
                if n_keys < c:
                    k_o = jnp.concatenate([k_o, jnp.zeros((c - n_keys, _DK), _BF16)], axis=0)
                off.append(lax.dot_general(q_o, k_o, trans_b, preferred_element_type=_F32))

        b_next = decay_of_gate(x_next)

        tiles = []
        for blk in range(n_sub):
            r0 = blk * sub
            sc = jnp.zeros((sub, c), _F32)
            for j in range(sub):
                sc = jnp.where(sub_col == r0 + j, diag[blk][j * sub:(j + 1) * sub], sc)
            sc = jnp.where(sub_row + r0 >= sub_col, sc, 0.0)
            if blk > 0:
                sc = sc + jnp.where(sub_col < r0, off[blk], 0.0)
            tiles.append(sc)
        scores = jnp.concatenate(tiles, axis=0) if n_sub > 1 else tiles[0]
        o_intra = jnp.dot(scores.astype(_BF16), v_ref[rows, :], preferred_element_type=_F32)
        bnext_scr[...] = b_next

        k_dec = (k_ref[rows, :].astype(_F32) * jnp.exp2(b_last - b_scr[...])).astype(_BF16)
        decay_col = jnp.transpose(jnp.broadcast_to(jnp.exp2(b_last), (_LANES, _DK)))
        for t in range(_DV // state_tile):
            cols = slice(t * state_tile, (t + 1) * state_tile)
            upd = lax.dot_general(k_dec, v_ref[rows, cols], (((0,), (0,)), ((), ())), preferred_element_type=_F32)
            s_scr[:, cols] = s_scr[:, cols] * jnp.tile(decay_col, (1, state_tile // _LANES)) + upd

        o = o_scr[...] + o_intra
        ms = jnp.mean(o * o, axis=-1, keepdims=True)
        o_n = o * lax.rsqrt(ms + _EPS) * hnorm_ref[...]
        r = r_ref[rows, :].astype(_F32)
        o_ref[rows, :] = (o_n * (r * jax.nn.sigmoid(r))).astype(o_ref.dtype)
        return carry

    bnext_scr[...] = decay_of_gate(gate_of_chunk(0))
    lax.fori_loop(0, n_chunks, one_chunk, 0)

    @pl.when(step == pl.num_programs(2) - 1)
    def _store_state():
        sfin_ref[0, 0] = s_scr[...]


def _gla(proj, a_low, wa2, ba, hnorm, s0, *, n_batch, seq, chunk, rows_per_step):
    m = proj.shape[0]
    assert m == n_batch * seq and seq % rows_per_step == 0 and rows_per_step % chunk == 0
    steps = seq // rows_per_step
    r = rows_per_step
    kv = _QK // _DV

    def row(b, h, i):
        return b * steps + i

    kernel = functools.partial(_gla_kernel, chunk=chunk, n_chunks=r // chunk, sub=min(chunk, 8))
    blocks = [r * _DK * 2, r * _DK * 2, r * _DV * 2, r * _DV * 2, r * _LANES * 4, _LANES * _DK * 2,
              _DK * 4, _DV * 4, _DK * _DV * 4, r * _DV * 2, _DK * _DV * 4]
    outs = pl.pallas_call(
        kernel,
        grid=(n_batch, _HEADS, steps),
        in_specs=[
            pl.BlockSpec((r, _DK), lambda b, h, i: (row(b, h, i), h)),
            pl.BlockSpec((r, _DK), lambda b, h, i: (row(b, h, i), _HEADS + h)),
            pl.BlockSpec((r, _DV), lambda b, h, i: (row(b, h, i), 2 * kv + h)),
            pl.BlockSpec((r, _DV), lambda b, h, i: (row(b, h, i), 2 * kv + _HEADS + h)),
            pl.BlockSpec((r, _LANES), lambda b, h, i: (row(b, h, i), 0)),
            pl.BlockSpec((_LANES, _DK), lambda b, h, i: (0, h)),
            pl.BlockSpec((1, _DK), lambda b, h, i: (0, h)),
            pl.BlockSpec((1, _DV), lambda b, h, i: (0, 0)),
            pl.BlockSpec((1, _DK, _DV), lambda b, h, i: (h, 0, 0)),
        ],
        out_specs=[
            pl.BlockSpec((r, _DV), lambda b, h, i: (row(b, h, i), h)),
            pl.BlockSpec((1, 1, _DK, _DV), lambda b, h, i: (b, h, 0, 0)),
        ],
        out_shape=[
            jax.ShapeDtypeStruct((m, _V), _BF16),
            jax.ShapeDtypeStruct((n_batch, _HEADS, _DK, _DV), _F32),
        ],
        scratch_shapes=[pltpu.VMEM((_DK, _DV), _F32), pltpu.VMEM((chunk, _DK), _F32), pltpu.VMEM((chunk, _DK), _F32),
                        pltpu.VMEM((chunk, _DV), _F32)],
        compiler_params=_params(blocks, scratch_bytes=4 * _DK * _DV * 4, n_grid=3),
        name="gla",
    )(proj, proj, proj, proj, a_low, wa2, ba.reshape(1, _QK), hnorm.reshape(1, _DV), s0)
    return outs[0], outs[1]


def _conv_in_kernel(a_ref, wb_ref, wc_ref, wu_ref, s_ref, am_ref, sm_ref, cw_ref, o_ref, tail_scr, meta_scr,
                    *, blocks_per_seq, row_split):
    i, j = pl.program_id(0), pl.program_id(1)

    @pl.when(i == 0)
    def _meta_rows():
        am, sm = am_ref[...], sm_ref[...]
        z_meta = ((jnp.dot(am, wc_ref[0], preferred_element_type=_F32) * sm)
                  * (jnp.dot(am, wu_ref[0], preferred_element_type=_F32) * sm))
        meta_scr[j] = z_meta[_N_META - _SUBLANES:]
        tail_scr[j] = z_meta[_N_META - _SUBLANES:]

    prev = jnp.where(i % blocks_per_seq == 0, meta_scr[j], tail_scr[j])
    cw = cw_ref[...]
    sub_rows = o_ref.shape[0] // row_split
    rid = lax.broadcasted_iota(jnp.int32, (sub_rows, o_ref.shape[1]), 0)
    for h in range(row_split):
        rows = slice(h * sub_rows, (h + 1) * sub_rows)
        a, s = a_ref[rows, :], s_ref[rows, :]
        bg = jnp.dot(a, wb_ref[0], preferred_element_type=_F32) * s
        z = ((jnp.dot(a, wc_ref[0], preferred_element_type=_F32) * s)
             * (jnp.dot(a, wu_ref[0], preferred_element_type=_F32) * s))
        z_m1, z_m2 = prev[_SUBLANES - 1:_SUBLANES], prev[_SUBLANES - 2:_SUBLANES - 1]
        z1 = jnp.where(rid == 0, z_m1, pltpu.roll(z, 1, 0))
        z2 = jnp.where(rid == 0, z_m2, jnp.where(rid == 1, z_m1, pltpu.roll(z, 2, 0)))
        conv = cw[2:3] * z + cw[0:1] * z2 + cw[1:2] * z1
        o_ref[rows, :] = (bg * conv).astype(o_ref.dtype)
        prev = z[sub_rows - _SUBLANES:]
    tail_scr[j] = prev


def _conv_in(a, rstd, a_meta, rstd_meta, w, conv_w, *, seq, bm, bn, row_split):
    m, k = a.shape
    assert m % bm == 0 and seq % bm == 0 and _D % bn == 0
    n_i, n_j = m // bm, _D // bn
    kernel = functools.partial(_conv_in_kernel, blocks_per_seq=seq // bm, row_split=row_split)
    blocks = [bm * k * 2, 3 * k * bn * 2, bm * _LANES * 4, _N_META * k * 2, _CONV_WIDTH * bn * 4, bm * bn * 2]
    return pl.pallas_call(
        kernel,
        grid=(n_i, n_j),
        in_specs=[
            pl.BlockSpec((bm, k), lambda i, j: (i, 0)),
            pl.BlockSpec((1, k, bn), lambda i, j: (j, 0, 0)),
            pl.BlockSpec((1, k, bn), lambda i, j: (n_j + j, 0, 0)),
            pl.BlockSpec((1, k, bn), lambda i, j: (2 * n_j + j, 0, 0)),
            pl.BlockSpec((bm, 1), lambda i, j: (i, 0)),
            pl.BlockSpec((_N_META, k), lambda i, j: (0, 0)),
            pl.BlockSpec((_N_META, 1), lambda i, j: (0, 0)),
            pl.BlockSpec((_CONV_WIDTH, bn), lambda i, j: (0, j)),
        ],
        out_specs=pl.BlockSpec((bm, bn), lambda i, j: (i, j)),
        out_shape=jax.ShapeDtypeStruct((m, _D), _BF16),
        scratch_shapes=[pltpu.VMEM((n_j, _SUBLANES, bn), _F32), pltpu.VMEM((n_j, _SUBLANES, bn), _F32)],
        compiler_params=_params(blocks, scratch_bytes=8 * (bm // row_split) * bn * 4),
        name="conv_in",
    )(a, w, w, w, rstd, a_meta, rstd_meta, conv_w)


def kernel(x, meta, norm_mix, norm_ffn, gla_w_in, gla_w_a2, gla_b_a, gla_head_norm, gla_w_out,
           conv_w_in, conv_w, conv_w_out, ffn_w_gate, ffn_w_up, ffn_w_down, norm_final):
    n_batch, seq, d = x.shape
    xm = x.reshape(n_batch * seq, d)
    xmeta = meta.astype(x.dtype)
    n_qkvr = 2 * _QK + 2 * _V
    d_ff = ffn_w_gate.shape[2]
    assert d_ff % 256 == 0


    w_in_t = jnp.transpose(gla_w_in, (0, 2, 1))
    w_qkvr = _cast_rows(w_in_t, 0, norm_mix[0], 0, n_qkvr, tile=1024, rb=512)
    w_a1 = _cast_rows(w_in_t, 0, norm_mix[0], n_qkvr, _RANK, tile=_LANES, rb=_LANES)
    w_a2 = jnp.pad(gla_w_a2[0], ((0, _LANES - _RANK), (0, 0))).astype(_BF16)

    hb, rstd = _row_stats(xm)
    hb_meta, rstd_meta = _row_stats(xmeta)
    proj, proj_meta, (w_out0, w_gate0, w_up0) = _matmul(
        hb, [(w_qkvr, 0)], n_qkvr, _epilogue_cast, _BF16, bm=1024, bn=1024, scale=(rstd, rstd_meta),
        meta=(hb_meta, None),
        jobs=[_CastJob(gla_w_out, 0, 64, 512), _CastJob(ffn_w_gate, 0, 128, 256, norm_ffn[0]),
              _CastJob(ffn_w_up, 0, 128, 256, norm_ffn[0])],
        row_split=4, name="gla_in")
    a_low, a_low_meta, _ = _matmul(hb, [(w_a1, 0)], _LANES, _epilogue_cast, _F32, bm=2048, bn=_LANES,
                                   scale=(rstd, rstd_meta), meta=(hb_meta, None), name="gla_gate_in")

    zero_state = jnp.zeros((_HEADS, _DK, _DV), _F32)
    o_meta, s_meta = _gla(proj_meta, a_low_meta, w_a2, gla_b_a[0], gla_head_norm[0], zero_state,
                          n_batch=1, seq=_N_META, chunk=_N_META, rows_per_step=_N_META)
    o_x, _ = _gla(proj, a_low, w_a2, gla_b_a[0], gla_head_norm[0], s_meta[0],
                  n_batch=n_batch, seq=seq, chunk=128, rows_per_step=2048)
    h, hb, rstd, h_meta, (w_down0,) = _matmul(
        o_x, [(w_out0, 0)], d, _epilogue_resid, _F32, bm=1024, bn=512, resid=xm, meta=(o_meta, xmeta),
        emit_stats=True, jobs=[_CastJob(ffn_w_down, 0, d_ff // 128, 512)], row_split=4, name="gla_out")

    hb_meta, rstd_meta = _row_stats(h_meta)
    act, act_meta, (w_cout, w_gate1, w_up1, w_cin) = _matmul(
        hb, [(w_gate0, 0), (w_up0, 0)], d_ff, _epilogue_swiglu, _BF16, bm=2048, bn=256, scale=(rstd, rstd_meta),
        meta=(hb_meta, None),
        jobs=[_CastJob(conv_w_out, 0, 64, 512), _CastJob(ffn_w_gate, 1, 256, 256, norm_ffn[1]),
              _CastJob(ffn_w_up, 1, 256, 256, norm_ffn[1]), _CastJob(conv_w_in, 0, 256, 256, norm_mix[1])],
        row_split=8, name="ffn_gate_up")
    h, hb, rstd, h_meta, (w_down1,) = _matmul(
        act, [(w_down0, 0)], d, _epilogue_resid, _F32, bm=512, bn=512, resid=h, meta=(act_meta, h_meta),
        emit_stats=True, jobs=[_CastJob(ffn_w_down, 1, d_ff // 64, 512)], row_split=2, name="ffn_down")

    hb_meta, rstd_meta = _row_stats(h_meta)
    y = _conv_in(hb, rstd, hb_meta, rstd_meta, w_cin, conv_w[0], seq=seq, bm=2048, bn=256, row_split=8)
    h, hb, rstd, _, _ = _matmul(y, [(w_cout, 0)], d, _epilogue_resid, _F32, bm=1024, bn=512, resid=h,
                                emit_stats=True, row_split=4, name="conv_out")

    act, _, _ = _matmul(hb, [(w_gate1, 0), (w_up1, 0)], d_ff, _epilogue_swiglu, _BF16, bm=2048, bn=256,
                        scale=(rstd, None), row_split=8, name="ffn_gate_up")
    h, _, _ = _matmul(act, [(w_down1, 0)], d, _epilogue_resid, _F32, bm=512, bn=512, resid=h, row_split=2,
                      name="ffn_down")
    out = _rmsnorm(h, norm_final, _F32)
    return out.reshape(n_batch, seq, d)
```
